```python
import math
import jax, jax.numpy as jnp
from jax import lax
import numpy as np

D_MODEL = 1024
BATCH = 8
SEQ = 4096
DEPTH = 4

RW_HEADS = 8
RW_HEAD = 64
RW_DIM = RW_HEADS * RW_HEAD
DECAY_LORA = 64
AAA_LORA = 64
GATE_LORA = 128
GN_EPS = 64e-5
MLA_HEADS = 8
QK_NOPE = 64
QK_ROPE = 32
V_HEAD = 64
Q_LORA = 384
KV_LORA = 256
MLA_DIM = MLA_HEADS * V_HEAD
ROPE_THETA = 10000.0
Q_BLOCK = 128
SHIFT_DIM = 3 * RW_DIM + DECAY_LORA + AAA_LORA + GATE_LORA
REST_DIM = Q_LORA + KV_LORA + QK_ROPE + 2 * D_MODEL
D_IN = SHIFT_DIM + REST_DIM
D_FF = 2816
N_EXPERTS = 8
TOP_K = 2
D_FF_EXPERT = 3584
EXPERT_BLOCK = 256
ALPHA = (2.0 * DEPTH) ** 0.25
BETA = (8.0 * DEPTH) ** -0.25
LN_EPS = 1e-5
RMS_EPS = 1e-6

kernel_name = 'rwkv7_mla_gated_hybrid_deepnorm_moe'


def split_last(z, sizes):
    idx, acc = [], 0
    for s in sizes[:-1]:
        acc += s
        idx.append(acc)
    return jnp.split(z, idx, axis=-1)


def layer_norm(z):
    zf = z.astype(jnp.float32)
    mu = jnp.mean(zf, -1, keepdims=True)
    var = jnp.mean(jnp.square(zf - mu), -1, keepdims=True)
    return ((zf - mu) * lax.rsqrt(var + LN_EPS)).astype(z.dtype)


def rms_norm(z, w):
    zf = z.astype(jnp.float32)
    y = zf * lax.rsqrt(jnp.mean(zf * zf, -1, keepdims=True) + RMS_EPS)
    return (y * w).astype(z.dtype)


def token_shift(z):
    return jnp.pad(z, ((0, 0), (1, 0), (0, 0)))[:, :-1]


def rope_cos_sin(positions, dtype):
    half = QK_ROPE // 2
    inv_freq = jnp.exp(-math.log(ROPE_THETA) * jnp.arange(half, dtype=jnp.float32) / half)
    ang = positions.astype(jnp.float32)[..., None] * inv_freq
    return jnp.cos(ang).astype(dtype), jnp.sin(ang).astype(dtype)


def apply_rope(z, cos, sin):
    z1, z2 = jnp.split(z, 2, axis=-1)
    return jnp.concatenate([z1 * cos - z2 * sin, z2 * cos + z1 * sin], axis=-1)


def rwkv7_branch(r, k, v, dw, da, dg, decay_w0, decay_up, aaa_a0, aaa_up, gate_up,
                 k_k, k_a, r_k, gn_w, gn_b):
    B, S, _ = r.shape
    f32 = jnp.float32
    w_log = -jax.nn.softplus(-(decay_w0 + jnp.tanh(dw) @ decay_up).astype(f32)) - 0.5
    decay = jnp.exp(-jnp.exp(w_log))
    a = jax.nn.sigmoid((aaa_a0 + da @ aaa_up).astype(f32))
    g = jax.nn.sigmoid(dg) @ gate_up
    heads = lambda z: z.astype(f32).reshape(B, S, RW_HEADS, RW_HEAD)
    kk = heads(k * k_k)
    kk = kk * lax.rsqrt(jnp.maximum(jnp.sum(kk * kk, -1, keepdims=True), 1e-24))
    k_mod = k.astype(f32) * (1.0 + (a - 1.0) * k_a)
    r_h, w_h, k_h, v_h, a_h = heads(r), heads(decay), heads(k_mod), heads(v), heads(a)
    tm = lambda z: jnp.swapaxes(z, 0, 1)

    def step(state, inp):
        r_t, w_t, k_t, v_t, kk_t, a_t = inp
        s_kk = jnp.einsum('bhvk,bhk->bhv', state, kk_t)
        state = (state * w_t[:, :, None, :]
                 - s_kk[..., None] * (kk_t * a_t)[:, :, None, :]
                 + v_t[..., None] * k_t[:, :, None, :])
        y_t = jnp.einsum('bhvk,bhk->bhv', state, r_t)
        return state, y_t

    state0 = jnp.zeros((B, RW_HEADS, RW_HEAD, RW_HEAD), f32)
    _, y = lax.scan(step, state0, (tm(r_h), tm(w_h), tm(k_h), tm(v_h), tm(kk), tm(a_h)))
    y = tm(y)
    mu = jnp.mean(y, -1, keepdims=True)
    var = jnp.mean(jnp.square(y - mu), -1, keepdims=True)
    y = ((y - mu) * lax.rsqrt(var + GN_EPS)).reshape(B, S, RW_DIM) * gn_w + gn_b
    bonus = jnp.sum(r_h * k_h * r_k, -1, keepdims=True) * v_h
    y = (y + bonus.reshape(B, S, RW_DIM)) * g.astype(f32)
    return y.astype(r.dtype)


def mla_branch(c_q, c_kv, k_rope, positions, q_norm_w, w_uq, kv_norm_w, w_ukv):
    B, S, _ = c_q.shape
    f32 = jnp.float32
    q = (rms_norm(c_q, q_norm_w) @ w_uq).reshape(B, S, MLA_HEADS, QK_NOPE + QK_ROPE)
    q_nope, q_rope = q[..., :QK_NOPE], q[..., QK_NOPE:]
    kv = (rms_norm(c_kv, kv_norm_w) @ w_ukv).reshape(B, S, MLA_HEADS, QK_NOPE + V_HEAD)
    k_nope, v = kv[..., :QK_NOPE], kv[..., QK_NOPE:]
    cos, sin = rope_cos_sin(positions, q.dtype)
    q_rope = apply_rope(q_rope, cos[:, :, None, :], sin[:, :, None, :])
    k_rope = apply_rope(k_rope, cos, sin)
    scale = (QK_NOPE + QK_ROPE) ** -0.5
    n_blocks = S // Q_BLOCK
    to_blocks = lambda z: jnp.moveaxis(z.reshape(B, n_blocks, Q_BLOCK, *z.shape[2:]), 1, 0)
    key_idx = jnp.arange(S)

    def attend(args):
        qn, qr, blk = args
        s = (jnp.einsum('bqhd,bkhd->bhqk', qn, k_nope)
             + jnp.einsum('bqhd,bkd->bhqk', qr, k_rope)).astype(f32) * scale
        q_idx = blk * Q_BLOCK + jnp.arange(Q_BLOCK)
        s = jnp.where(key_idx[None, :] <= q_idx[:, None], s, -1e30)
        p = jax.nn.softmax(s, axis=-1).astype(v.dtype)
        return jnp.einsum('bhqk,bkhd->bqhd', p, v)

    out = lax.map(attend, (to_blocks(q_nope), to_blocks(q_rope), jnp.arange(n_blocks)))
    return jnp.moveaxis(out, 0, 1).reshape(B, S, MLA_DIM)


def token_mixer(u, positions, w_in, shift_mu, decay_w0, decay_up, aaa_a0, aaa_up, gate_up,
                k_k, k_a, r_k, gn_w, gn_b, q_norm_w, w_uq, kv_norm_w, w_ukv, p_a, p_b, w_o):
    proj = u @ w_in
    p_shift = proj[..., :SHIFT_DIM]
    p_shift = p_shift + (token_shift(p_shift) - p_shift) * shift_mu
    r, k, v, dw, da, dg = split_last(p_shift, [RW_DIM, RW_DIM, RW_DIM, DECAY_LORA, AAA_LORA, GATE_LORA])
    c_q, c_kv, k_rope, gate_a, gate_b = split_last(proj[..., SHIFT_DIM:],
                                                   [Q_LORA, KV_LORA, QK_ROPE, D_MODEL, D_MODEL])
    y_a = rwkv7_branch(r, k, v, dw, da, dg, decay_w0, decay_up, aaa_a0, aaa_up, gate_up,
                       k_k, k_a, r_k, gn_w, gn_b)
    y_b = mla_branch(c_q, c_kv, k_rope, positions, q_norm_w, w_uq, kv_norm_w, w_ukv)
    merged = jax.nn.sigmoid(gate_a) * (y_a @ p_a) + jax.nn.sigmoid(gate_b) * (y_b @ p_b)
    return merged @ w_o


def swiglu(h, w_gate, w_up, w_down):
    return (jax.nn.silu(h @ w_gate) * (h @ w_up)) @ w_down


def moe_swiglu(h, router_w, router_b, w_gate, w_up, w_down):
    B, S, D = h.shape
    N = B * S
    M = N * TOP_K
    ht = h.reshape(N, D)
    logits = (ht @ router_w).astype(jnp.float32) + router_b.astype(jnp.float32)
    top_logit, top_idx = lax.top_k(logits, TOP_K)
    top_gate = jax.nn.softmax(top_logit, axis=-1)
    flat_e = top_idx.reshape(M)
    flat_tok = jnp.arange(M, dtype=jnp.int32) // TOP_K
    flat_gate = top_gate.reshape(M)
    order = jnp.argsort(flat_e)
    e_sorted = flat_e[order]
    counts = jnp.bincount(flat_e, length=N_EXPERTS).astype(jnp.int32)
    padded = (counts + EXPERT_BLOCK - 1) // EXPERT_BLOCK * EXPERT_BLOCK
    start = jnp.cumsum(counts) - counts
    pad_end = jnp.cumsum(padded)
    pad_start = pad_end - padded
    dest = pad_start[e_sorted] + jnp.arange(M, dtype=jnp.int32) - start[e_sorted]
    n_blocks = -(-M // EXPERT_BLOCK) + N_EXPERTS
    m_pad = n_blocks * EXPERT_BLOCK
    slot_tok = jnp.full((m_pad,), N, jnp.int32).at[dest].set(flat_tok[order])
    slot_gate = jnp.zeros((m_pad,), jnp.float32).at[dest].set(flat_gate[order])
    block_e = jnp.minimum(jnp.searchsorted(pad_end, jnp.arange(n_blocks) * EXPERT_BLOCK, side='right'),
                          N_EXPERTS - 1)
    h_ext = jnp.concatenate([ht, jnp.zeros((1, D), ht.dtype)], axis=0)
    xs = h_ext[slot_tok].reshape(n_blocks, EXPERT_BLOCK, D)

    def expert_block(args):
        xb, e = args
        return swiglu(xb, w_gate[e], w_up[e], w_down[e])

    ys = lax.map(expert_block, (xs, block_e)).reshape(m_pad, D)
    ys = ys * slot_gate[:, None].astype(ys.dtype)
    out = jnp.zeros((N + 1, D), ys.dtype).at[slot_tok].add(ys)[:N]
    return out.reshape(B, S, D)


def setup_inputs(seed: int = 0) -> dict:
    key = jax.random.key(seed)
    ks = iter(jax.random.split(key, 48))
    f32 = jnp.float32
    L = DEPTH
    LD = (DEPTH + 1) // 2
    LM = DEPTH // 2
    nrm = lambda shape, s: jax.random.normal(next(ks), shape, f32) * s
    unif = lambda shape, lo, hi: jax.random.uniform(next(ks), shape, f32, lo, hi)
    x = nrm((BATCH, SEQ, D_MODEL), 1.0)
    c = nrm((BATCH, D_MODEL), 1.0)
    positions = (jax.random.randint(next(ks), (BATCH, 1), 0, 1024, jnp.int32)
                 + jnp.arange(SEQ, dtype=jnp.int32)[None, :])
    return {
        'x': x,
        'c': c,
        'positions': positions,
        'w_ada': nrm((L, D_MODEL, 6 * D_MODEL), 0.5 * D_MODEL ** -0.5),
        'b_ada': nrm((L, 6 * D_MODEL), 0.02),
        'w_in': nrm((L, D_MODEL, D_IN), D_MODEL ** -0.5),
        'shift_mu': unif((L, SHIFT_DIM), 0.0, 1.0),
        'decay_w0': unif((L, RW_DIM), -5.0, 0.0),
        'decay_up': nrm((L, DECAY_LORA, RW_DIM), 0.1 * DECAY_LORA ** -0.5),
        'aaa_a0': nrm((L, RW_DIM), 0.1),
        'aaa_up': nrm((L, AAA_LORA, RW_DIM), AAA_LORA ** -0.5),
        'gate_up': nrm((L, GATE_LORA, RW_DIM), GATE_LORA ** -0.5),
        'k_k': 0.85 + nrm((L, RW_DIM), 0.05),
        'k_a': 1.0 + nrm((L, RW_DIM), 0.05),
        'r_k': nrm((L, RW_HEADS, RW_HEAD), 0.1),
        'gn_w': 1.0 + nrm((L, RW_DIM), 0.05),
        'gn_b': nrm((L, RW_DIM), 0.02),
        'q_norm_w': 1.0 + nrm((L, Q_LORA), 0.05),
        'w_uq': nrm((L, Q_LORA, MLA_HEADS * (QK_NOPE + QK_ROPE)), Q_LORA ** -0.5),
        'kv_norm_w': 1.0 + nrm((L, KV_LORA), 0.05),
        'w_ukv': nrm((L, KV_LORA, MLA_HEADS * (QK_NOPE + V_HEAD)), KV_LORA ** -0.5),
        'p_a': nrm((L, RW_DIM, D_MODEL), RW_DIM ** -0.5),
        'p_b': nrm((L, MLA_DIM, D_MODEL), MLA_DIM ** -0.5),
        'w_o': nrm((L, D_MODEL, D_MODEL), BETA * D_MODEL ** -0.5),
        'ln1_w': 1.0 + nrm((L, D_MODEL), 0.05),
        'ln1_b': nrm((L, D_MODEL), 0.02),
        'ln2_w': 1.0 + nrm((L, D_MODEL), 0.05),
        'ln2_b': nrm((L, D_MODEL), 0.02),
        'ffn_w_gate': nrm((LD, D_MODEL, D_FF), D_MODEL ** -0.5),
        'ffn_w_up': nrm((LD, D_MODEL, D_FF), D_MODEL ** -0.5),
        'ffn_w_down': nrm((LD, D_FF, D_MODEL), BETA * D_FF ** -0.5),
        'router_w': nrm((LM, D_MODEL, N_EXPERTS), D_MODEL ** -0.5),
        'router_b': nrm((LM, N_EXPERTS), 0.01),
        'moe_w_gate': nrm((LM, N_EXPERTS, D_MODEL, D_FF_EXPERT), D_MODEL ** -0.5),
        'moe_w_up': nrm((LM, N_EXPERTS, D_MODEL, D_FF_EXPERT), D_MODEL ** -0.5),
        'moe_w_down': nrm((LM, N_EXPERTS, D_FF_EXPERT, D_MODEL), BETA * D_FF_EXPERT ** -0.5),
    }


def reference(x, c, positions, w_ada, b_ada, w_in, shift_mu, decay_w0, decay_up, aaa_a0, aaa_up,
              gate_up, k_k, k_a, r_k, gn_w, gn_b, q_norm_w, w_uq, kv_norm_w, w_ukv, p_a, p_b, w_o,
              ln1_w, ln1_b, ln2_w, ln2_b, ffn_w_gate, ffn_w_up, ffn_w_down, router_w, router_b,
              moe_w_gate, moe_w_up, moe_w_down):
    c_act = jax.nn.silu(c)
    for l in range(DEPTH):
        mod = (c_act @ w_ada[l] + b_ada[l])[:, None, :]
        sh1, sc1, g1, sh2, sc2, g2 = jnp.split(mod, 6, axis=-1)
        u = layer_norm(x) * (1.0 + sc1) + sh1
        mix = token_mixer(u, positions, w_in[l], shift_mu[l], decay_w0[l], decay_up[l], aaa_a0[l],
                          aaa_up[l], gate_up[l], k_k[l], k_a[l], r_k[l], gn_w[l], gn_b[l],
                          q_norm_w[l], w_uq[l], kv_norm_w[l], w_ukv[l], p_a[l], p_b[l], w_o[l])
        x = layer_norm(ALPHA * x + g1 * mix) * ln1_w[l] + ln1_b[l]
        h = layer_norm(x) * (1.0 + sc2) + sh2
        if l % 2 == 0:
            f = swiglu(h, ffn_w_gate[l // 2], ffn_w_up[l // 2], ffn_w_down[l // 2])
        else:
            f = moe_swiglu(h, router_w[l // 2], router_b[l // 2], moe_w_gate[l // 2],
                           moe_w_up[l // 2], moe_w_down[l // 2])
        x = layer_norm(ALPHA * x + g2 * f) * ln2_w[l] + ln2_b[l]
    return x
```

```python
import functools
import math

import jax
import jax.numpy as jnp
from jax import lax
from jax.experimental import pallas as pl
from jax.experimental.pallas import tpu as pltpu

F32 = jnp.float32
BF16 = jnp.bfloat16

D_MODEL = 1024
DEPTH = 4
RW_HEADS = 8
RW_HEAD = 64
RW_DIM = RW_HEADS * RW_HEAD
DECAY_LORA = 64
AAA_LORA = 64
GATE_LORA = 128
GN_EPS = 64e-5
MLA_HEADS = 8
QK_NOPE = 64
QK_ROPE = 32
V_HEAD = 64
Q_LORA = 384
KV_LORA = 256
MLA_DIM = MLA_HEADS * V_HEAD
ROPE_THETA = 10000.0
SHIFT_DIM = 3 * RW_DIM + DECAY_LORA + AAA_LORA + GATE_LORA
LAT_DIM = Q_LORA + KV_LORA + QK_ROPE
LAT_PAD = 768
D_FF = 2816
N_EXPERTS = 8
TOP_K = 2
D_FF_EXPERT = 3584
ALPHA = (2.0 * DEPTH) ** 0.25
LN_EPS = 1e-5
RMS_EPS = 1e-6
LANES = 128
HEAD_PAD = 128
NEG_BIG = -1e30

RW_CHUNK = 64
TM_MM = 512
TM_PREP = 256
TQ = 256
TM_MERGE = 256
TM_FFN = 512
FF_CHUNK = 256
TM_ROUTE = 512
TM_SCAT = 512
TM_EXP = 512
TF_EXP = 512
TM_COMB = 256

_HI = lax.Precision.HIGHEST


def _cparams(*sem):
    return pltpu.CompilerParams(dimension_semantics=sem)


def _dot(a, b):
    return jnp.dot(a.astype(BF16), b.astype(BF16), preferred_element_type=F32)


def _dot_nt(a, b):
    return lax.dot_general(a.astype(BF16), b.astype(BF16), (((1,), (1,)), ((), ())),
                           preferred_element_type=F32)


def _dot_tn(a, b):
    return lax.dot_general(a.astype(BF16), b.astype(BF16), (((0,), (0,)), ((), ())),
                           preferred_element_type=F32)


def _split(x, n):
    parts = []
    for _ in range(n - 1):
        p = x.astype(BF16)
        parts.append(p)
        x = x - p.astype(F32)
    parts.append(x.astype(BF16))
    return parts


def _dot_split_lhs(x, w_bf16, n):
    acc = None
    for p in _split(x, n):
        t = jnp.dot(p, w_bf16, preferred_element_type=F32)
        acc = t if acc is None else acc + t
    return acc


def _dot_split_rhs(w_bf16, x, n):
    acc = None
    for p in _split(x, n):
        t = jnp.dot(w_bf16, p, preferred_element_type=F32)
        acc = t if acc is None else acc + t
    return acc


def _layer_norm(x):
    mu = jnp.mean(x, -1, keepdims=True)
    xc = x - mu
    var = jnp.mean(xc * xc, -1, keepdims=True)
    return xc * lax.rsqrt(var + LN_EPS)


def _full(shape):
    nd = len(shape)
    return pl.BlockSpec(shape, lambda *_: (0,) * nd)


def _ada_kernel(c_ref, w_ref, b_ref, o_ref):
    c = c_ref[...]
    ca = c * jax.nn.sigmoid(c)
    o_ref[0] = jnp.dot(ca, w_ref[0], preferred_element_type=F32, precision=_HI) + b_ref[0]


def _ada(c, w_ada, b_ada):
    L, _, six_d = w_ada.shape
    B = c.shape[0]
    nj = six_d // D_MODEL
    return pl.pallas_call(
        _ada_kernel,
        grid=(L, nj),
        in_specs=[
            pl.BlockSpec((B, D_MODEL), lambda l, j: (0, 0)),
            pl.BlockSpec((1, D_MODEL, D_MODEL), lambda l, j: (l, 0, j)),
            pl.BlockSpec((1, 1, D_MODEL), lambda l, j: (l, 0, j)),
        ],
        out_specs=pl.BlockSpec((1, B, D_MODEL), lambda l, j: (l, 0, j)),
        out_shape=jax.ShapeDtypeStruct((L, B, six_d), F32),
        compiler_params=_cparams("parallel", "parallel"),
        name="ada",
    )(c, w_ada, b_ada.reshape(L, 1, six_d))


def _lnmod_kernel(x_ref, mod_ref, o_ref):
    m = mod_ref[0]
    o_ref[0] = (_layer_norm(x_ref[0]) * (1.0 + m[1:2]) + m[0:1]).astype(o_ref.dtype)


def _lnmod(x, mod):
    B, S, _ = x.shape
    tm = min(TM_MM, S)
    return pl.pallas_call(
        _lnmod_kernel,
        grid=(B, S // tm),
        in_specs=[
            pl.BlockSpec((1, tm, D_MODEL), lambda b, i: (b, i, 0)),
            pl.BlockSpec((1, 6, D_MODEL), lambda b, i: (b, 0, 0)),
        ],
        out_specs=pl.BlockSpec((1, tm, D_MODEL), lambda b, i: (b, i, 0)),
        out_shape=jax.ShapeDtypeStruct((B, S, D_MODEL), BF16),
        compiler_params=_cparams("parallel", "parallel"),
        name="lnmod",
    )(x, mod)


def _mm_kernel(a_ref, w_ref, o_ref, *, act):
    acc = jnp.dot(a_ref[...], w_ref[...], preferred_element_type=F32)
    if act == "sigmoid":
        acc = jax.nn.sigmoid(acc)
    o_ref[...] = acc.astype(o_ref.dtype)


def _mm(a, w, out_dtype, act=None):
    n, k = a.shape
    n_out = w.shape[1]
    tm = min(TM_MM, n)
    return pl.pallas_call(
        functools.partial(_mm_kernel, act=act),
        grid=(n // tm,),
        in_specs=[
            pl.BlockSpec((tm, k), lambda i: (i, 0)),
            pl.BlockSpec((k, n_out), lambda i: (0, 0)),
        ],
        out_specs=pl.BlockSpec((tm, n_out), lambda i: (i, 0)),
        out_shape=jax.ShapeDtypeStruct((n, n_out), out_dtype),
        compiler_params=_cparams("parallel"),
        name="mm",
    )(a, w)


def _rwkv_kernel(ps_ref, mu_ref, w0_ref, dup_ref, a0_ref, aup_ref, gup_ref, kk_ref, ka_ref, rk_ref,
                 gnw_ref, gnb_ref, hsum_ref, o_ref, state_ref, prev_ref, y_ref):
    C = RW_CHUNK
    N = RW_HEAD

    @pl.when(pl.program_id(1) == 0)
    def _init():
        state_ref[...] = jnp.zeros_like(state_ref)
        prev_ref[...] = jnp.zeros_like(prev_ref)

    p = ps_ref[0]
    row = lax.broadcasted_iota(jnp.int32, p.shape, 0)
    shifted = jnp.where(row == 0, prev_ref[0:1, :], pltpu.roll(p, 1, 0))
    prev_ref[0:1, :] = p[C - 1:C, :]
    xs = p + (shifted - p) * mu_ref[...]
    r = xs[:, 0:RW_DIM]
    k = xs[:, RW_DIM:2 * RW_DIM]
    v = xs[:, 2 * RW_DIM:3 * RW_DIM]
    o = 3 * RW_DIM
    dw = xs[:, o:o + DECAY_LORA]
    da = xs[:, o + DECAY_LORA:o + DECAY_LORA + AAA_LORA]
    dg = xs[:, o + DECAY_LORA + AAA_LORA:SHIFT_DIM]

    lw = -math.exp(-0.5) * jax.nn.sigmoid(w0_ref[...] + _dot(jnp.tanh(dw), dup_ref[...]))
    a = jax.nn.sigmoid(a0_ref[...] + _dot(da, aup_ref[...]))
    g = _dot(jax.nn.sigmoid(dg), gup_ref[...])

    hsum = hsum_ref[...]
    kk = k * kk_ref[...]
    kk = kk * lax.rsqrt(jnp.maximum(_dot_split_lhs(kk * kk, hsum, 2), 1e-24))
    k_mod = k * (1.0 + (a - 1.0) * ka_ref[...])
    b = kk * a
    bonus = _dot_split_lhs(r * k_mod * rk_ref[...], hsum, 2) * v

    ri = lax.broadcasted_iota(jnp.int32, (C, C), 0)
    ci = lax.broadcasted_iota(jnp.int32, (C, C), 1)
    tri = (ci <= ri).astype(BF16)
    eye = (ci == ri).astype(F32)
    cw = _dot_split_rhs(tri, lw, 3)
    cw_last = cw[C - 1:C, :]
    e_neg = jnp.exp(-cw)
    kp = (kk * jnp.exp(cw - lw)).astype(BF16)
    rp = (r * jnp.exp(cw)).astype(BF16)
    bm = (b * e_neg).astype(BF16)
    km = (k_mod * e_neg).astype(BF16)
    e_rem = jnp.exp(cw_last - cw)
    bt = (b * e_rem).astype(BF16)
    kt = (k_mod * e_rem).astype(BF16)
    w_c = jnp.exp(cw_last)
    vb = v.astype(BF16)

    r2 = lax.broadcasted_iota(jnp.int32, (2 * C, 2 * C), 0)
    c2 = lax.broadcasted_iota(jnp.int32, (2 * C, 2 * C), 1)
    keep = (c2 % C) < (r2 % C) + (r2 >= C).astype(jnp.int32)

    for h in range(RW_HEADS):
        sl = slice(h * N, (h + 1) * N)
        kr = jnp.concatenate([kp[:, sl], rp[:, sl]], axis=0)
        bk = jnp.concatenate([bm[:, sl], km[:, sl]], axis=0)
        aa = jnp.where(keep, _dot_nt(kr, bk), 0.0)
        a_ab = aa[0:C, 0:C]
        t_inv = eye - a_ab
        a_pow = a_ab
        for _ in range(int(math.log2(C)) - 1):
            a_pow = _dot(a_pow, a_pow)
            t_inv = t_inv + _dot(t_inv, a_pow)
        s0 = state_ref[h]
        xx = _dot_nt(kr, s0)
        v_h = vb[:, sl]
        gmat = xx[0:C] + _dot(aa[0:C, C:2 * C], v_h)
        u = -_dot(t_inv, gmat)
        uv = jnp.concatenate([u.astype(BF16), v_h], axis=0)
        y_ref[:, sl] = xx[C:2 * C] + _dot(aa[C:2 * C, :], uv)
        btkt = jnp.concatenate([bt[:, sl], kt[:, sl]], axis=0)
        state_ref[h] = s0 * w_c[:, sl] + _dot_tn(uv, btkt)

    y = y_ref[...]
    mean = _dot_split_lhs(y, hsum, 2) * (1.0 / N)
    yc = y - mean
    var = _dot_split_lhs(yc * yc, hsum, 2) * (1.0 / N)
    yn = yc * lax.rsqrt(var + GN_EPS) * gnw_ref[...] + gnb_ref[...]
    o_ref[0] = ((yn + bonus) * g).astype(o_ref.dtype)


def _rwkv(ps, mu, w0, dup, a0, aup, gup, k_k, k_a, r_k, gn_w, gn_b, hsum):
    B, S, _ = ps.shape
    C = RW_CHUNK
    row = lambda n: pl.BlockSpec((1, n), lambda b, j: (0, 0))
    return pl.pallas_call(
        _rwkv_kernel,
        grid=(B, S // C),
        in_specs=[
            pl.BlockSpec((1, C, SHIFT_DIM), lambda b, j: (b, j, 0)),
            row(SHIFT_DIM), row(RW_DIM),
            pl.BlockSpec((DECAY_LORA, RW_DIM), lambda b, j: (0, 0)),
            row(RW_DIM),
            pl.BlockSpec((AAA_LORA, RW_DIM), lambda b, j: (0, 0)),
            pl.BlockSpec((GATE_LORA, RW_DIM), lambda b, j: (0, 0)),
            row(RW_DIM), row(RW_DIM), row(RW_DIM), row(RW_DIM), row(RW_DIM),
            pl.BlockSpec((RW_DIM, RW_DIM), lambda b, j: (0, 0)),
        ],
        out_specs=pl.BlockSpec((1, C, RW_DIM), lambda b, j: (b, j, 0)),
        out_shape=jax.ShapeDtypeStruct((B, S, RW_DIM), BF16),
        scratch_shapes=[
            pltpu.VMEM((RW_HEADS, RW_HEAD, RW_HEAD), F32),
            pltpu.VMEM((8, SHIFT_DIM), F32),
            pltpu.VMEM((C, RW_DIM), F32),
        ],
        compiler_params=_cparams("parallel", "arbitrary"),
        name="rwkv",
    )(ps, mu, w0, dup, a0, aup, gup, k_k, k_a, r_k, gn_w, gn_b, hsum)


def _mla_prep_kernel(lat_ref, pos_ref, qnw_ref, wq_ref, wqr_ref, kvnw_ref, wk_ref, wv_ref,
                     place_ref, placer_ref, q_ref, k_ref, v_ref):
    lat = lat_ref[0]
    c_q = lat[:, 0:Q_LORA]
    c_kv = lat[:, Q_LORA:Q_LORA + KV_LORA]
    k_rope = lat[:, Q_LORA + KV_LORA:LAT_DIM]

    def rms(z, w):
        return z * lax.rsqrt(jnp.mean(z * z, -1, keepdims=True) + RMS_EPS) * w

    half = QK_ROPE // 2
    lane = lax.broadcasted_iota(jnp.int32, (1, HEAD_PAD), 1)
    in_rope = (lane >= QK_NOPE) & (lane < QK_NOPE + QK_ROPE)
    fidx = jnp.where(lane < QK_NOPE + half, lane - QK_NOPE, lane - QK_NOPE - half)
    inv_freq = jnp.exp(-math.log(ROPE_THETA) * fidx.astype(F32) / half)
    ang = pos_ref[0] * inv_freq
    cos_r = jnp.where(in_rope, jnp.cos(ang), 0.0)
    sin_r = jnp.where(in_rope, jnp.sin(ang), 0.0)
    scale = (QK_NOPE + QK_ROPE) ** -0.5
    q_mul = jnp.where(lane < QK_NOPE, 1.0, cos_r) * scale
    q_mul_rot = sin_r * scale

    cq_n = rms(c_q, qnw_ref[...]).astype(BF16)
    q_all = jnp.dot(cq_n, wq_ref[...], preferred_element_type=F32)
    q_rot = jnp.dot(cq_n, wqr_ref[...], preferred_element_type=F32)
    ckv_n = rms(c_kv, kvnw_ref[...]).astype(BF16)
    k_all = jnp.dot(ckv_n, wk_ref[...], preferred_element_type=F32)
    v_all = jnp.dot(ckv_n, wv_ref[...], preferred_element_type=F32)
    kr_pad = (_dot_split_lhs(k_rope, place_ref[...], 3) * cos_r
              + _dot_split_lhs(k_rope, placer_ref[...], 3) * sin_r)
    for h in range(MLA_HEADS):
        sl = slice(h * HEAD_PAD, (h + 1) * HEAD_PAD)
        q_ref[0, h] = (q_all[:, sl] * q_mul + q_rot[:, sl] * q_mul_rot).astype(q_ref.dtype)
        k_ref[0, h] = (k_all[:, sl] + kr_pad).astype(k_ref.dtype)
    for hp in range(MLA_HEADS // 2):
        v_ref[0, hp] = v_all[:, hp * LANES:(hp + 1) * LANES].astype(v_ref.dtype)


def _mla_prep(lat, posf, qnw, wq, wqr, kvnw, wk, wv, place, placer):
    B, S, _ = lat.shape
    tm = min(TM_PREP, S)
    H = MLA_HEADS
    cst = lambda shape: pl.BlockSpec(shape, lambda b, i: (0,) * len(shape))
    return pl.pallas_call(
        _mla_prep_kernel,
        grid=(B, S // tm),
        in_specs=[
            pl.BlockSpec((1, tm, LAT_PAD), lambda b, i: (b, i, 0)),
            pl.BlockSpec((1, tm, 1), lambda b, i: (b, i, 0)),
            cst((1, Q_LORA)), cst((Q_LORA, H * HEAD_PAD)), cst((Q_LORA, H * HEAD_PAD)),
            cst((1, KV_LORA)), cst((KV_LORA, H * HEAD_PAD)), cst((KV_LORA, MLA_DIM)),
            cst((QK_ROPE, HEAD_PAD)), cst((QK_ROPE, HEAD_PAD)),
        ],
        out_specs=[
            pl.BlockSpec((1, H, tm, HEAD_PAD), lambda b, i: (b, 0, i, 0)),
            pl.BlockSpec((1, H, tm, HEAD_PAD), lambda b, i: (b, 0, i, 0)),
            pl.BlockSpec((1, H // 2, tm, LANES), lambda b, i: (b, 0, i, 0)),
        ],
        out_shape=[
            jax.ShapeDtypeStruct((B, H, S, HEAD_PAD), BF16),
            jax.ShapeDtypeStruct((B, H, S, HEAD_PAD), BF16),
            jax.ShapeDtypeStruct((B, H // 2, S, LANES), BF16),
        ],
        compiler_params=_cparams("parallel", "parallel"),
        name="mla_prep",
    )(lat, posf, qnw, wq, wqr, kvnw, wk, wv, place, placer)


def _attn_kernel(q_ref, k_ref, v_ref, o_ref, *, tq):
    i = pl.program_id(2)
    lane = lax.broadcasted_iota(jnp.int32, (tq, LANES), 1)
    ri = lax.broadcasted_iota(jnp.int32, (tq, tq), 0)
    ci = lax.broadcasted_iota(jnp.int32, (tq, tq), 1)
    outs = []
    for hh in range(2):
        q = q_ref[0, hh]

        def step(j, carry, masked):
            m, l, acc = carry
            start = pl.multiple_of(j * tq, tq)
            ks = k_ref[0, hh, pl.ds(start, tq), :]
            vs = v_ref[0, 0, pl.ds(start, tq), :]
            s = lax.dot_general(q, ks, (((1,), (1,)), ((), ())), preferred_element_type=F32)
            if masked:
                s = jnp.where(ci <= ri, s, NEG_BIG)
            m_new = jnp.maximum(m, jnp.max(s, -1, keepdims=True))
            alpha = jnp.exp(m - m_new)
            p = jnp.exp(s - m_new)
            l = alpha * l + jnp.sum(p, -1, keepdims=True)
            acc = alpha * acc + jnp.dot(p.astype(BF16), vs, preferred_element_type=F32)
            return m_new, l, acc

        init = (jnp.full((tq, 1), NEG_BIG, F32), jnp.zeros((tq, 1), F32), jnp.zeros((tq, LANES), F32))
        carry = lax.fori_loop(0, i, functools.partial(step, masked=False), init)
        m, l, acc = step(i, carry, True)
        outs.append(acc / l)
    o_ref[0] = jnp.where(lane < V_HEAD, outs[0], outs[1]).astype(o_ref.dtype)


def _attn(q, k, v):
    B, H, S, _ = q.shape
    tq = min(TQ, S)
    return pl.pallas_call(
        functools.partial(_attn_kernel, tq=tq),
        grid=(B, H // 2, S // tq),
        in_specs=[
            pl.BlockSpec((1, 2, tq, HEAD_PAD), lambda b, p, i: (b, p, i, 0)),
            pl.BlockSpec((1, 2, S, HEAD_PAD), lambda b, p, i: (b, p, 0, 0)),
            pl.BlockSpec((1, 1, S, LANES), lambda b, p, i: (b, p, 0, 0)),
        ],
        out_specs=pl.BlockSpec((1, tq, LANES), lambda b, p, i: (b, i, p)),
        out_shape=jax.ShapeDtypeStruct((B, S, MLA_DIM), BF16),
        compiler_params=_cparams("parallel", "parallel", "arbitrary"),
        name="attn",
    )(q, k, v)


def _residual_epilogue(x, f, gate, ln_w, ln_b):
    return _layer_norm(ALPHA * x + gate * f) * ln_w + ln_b


def _merge_out_kernel(x_ref, ya_ref, yb_ref, sga_ref, sgb_ref, pa_ref, pb_ref, wo_ref, mod_ref,
                      lnw_ref, lnb_ref, x1_ref, h_ref):
    m = mod_ref[0]
    ma = jnp.dot(ya_ref[0], pa_ref[...], preferred_element_type=F32)
    mb = jnp.dot(yb_ref[0], pb_ref[...], preferred_element_type=F32)
    merged = sga_ref[0].astype(F32) * ma + sgb_ref[0].astype(F32) * mb
    mix = jnp.dot(merged.astype(BF16), wo_ref[...], preferred_element_type=F32)
    x1 = _residual_epilogue(x_ref[0], mix, m[2:3], lnw_ref[...], lnb_ref[...])
    x1_ref[0] = x1
    h_ref[0] = (_layer_norm(x1) * (1.0 + m[4:5]) + m[3:4]).astype(h_ref.dtype)


def _merge_out(x, ya, yb, sg, pa, pb, wo, mod, ln_w, ln_b, h_dtype):
    B, S, _ = x.shape
    tm = min(TM_MERGE, S)
    cst = lambda shape: pl.BlockSpec(shape, lambda b, i: (0,) * len(shape))
    rows = lambda n: pl.BlockSpec((1, tm, n), lambda b, i: (b, i, 0))
    return pl.pallas_call(
        _merge_out_kernel,
        grid=(B, S // tm),
        in_specs=[
            rows(D_MODEL), rows(RW_DIM), rows(MLA_DIM),
            pl.BlockSpec((1, tm, D_MODEL), lambda b, i: (b, i, 0)),
            pl.BlockSpec((1, tm, D_MODEL), lambda b, i: (b, i, 1)),
            cst((RW_DIM, D_MODEL)), cst((MLA_DIM, D_MODEL)), cst((D_MODEL, D_MODEL)),
            pl.BlockSpec((1, 6, D_MODEL), lambda b, i: (b, 0, 0)),
            cst((1, D_MODEL)), cst((1, D_MODEL)),
        ],
        out_specs=[rows(D_MODEL), rows(D_MODEL)],
        out_shape=[
            jax.ShapeDtypeStruct((B, S, D_MODEL), F32),
            jax.ShapeDtypeStruct((B, S, D_MODEL), h_dtype),
        ],
        compiler_params=_cparams("parallel", "parallel"),
        name="merge_out",
    )(x, ya, yb, sg, sg, pa, pb, wo, mod, ln_w, ln_b)


def _write_epilogue(x1, f, m, modn_ref, lnw_ref, lnb_ref, x2_ref, u_ref):
    x2 = _residual_epilogue(x1, f, m[5:6], lnw_ref[...], lnb_ref[...])
    x2_ref[0] = x2
    mn = modn_ref[0]
    u_ref[0] = (_layer_norm(x2) * (1.0 + mn[1:2]) + mn[0:1]).astype(u_ref.dtype)


def _ffn_kernel(x1_ref, h_ref, wg_ref, wu_ref, wd_ref, mod_ref, modn_ref, lnw_ref, lnb_ref,
                x2_ref, u_ref):
    h = h_ref[0]
    acc = None
    for j in range(D_FF // FF_CHUNK):
        sl = slice(j * FF_CHUNK, (j + 1) * FF_CHUNK)
        gt = jnp.dot(h, wg_ref[:, sl], preferred_element_type=F32)
        up = jnp.dot(h, wu_ref[:, sl], preferred_element_type=F32)
        act = (gt * jax.nn.sigmoid(gt) * up).astype(BF16)
        t = jnp.dot(act, wd_ref[sl, :], preferred_element_type=F32)
        acc = t if acc is None else acc + t
    _write_epilogue(x1_ref[0], acc, mod_ref[0], modn_ref, lnw_ref, lnb_ref, x2_ref, u_ref)


def _ffn(x1, h, wg, wu, wd, mod, modn, ln_w, ln_b):
    B, S, _ = x1.shape
    tm = min(TM_FFN, S)
    res = lambda shape: pl.BlockSpec(shape, lambda b, i: (0,) * len(shape), pipeline_mode=pl.Buffered(1))
    cst = lambda shape: pl.BlockSpec(shape, lambda b, i: (0,) * len(shape))
    rows = pl.BlockSpec((1, tm, D_MODEL), lambda b, i: (b, i, 0))
    modspec = pl.BlockSpec((1, 6, D_MODEL), lambda b, i: (b, 0, 0))
    return pl.pallas_call(
        _ffn_kernel,
        grid=(B, S // tm),
        in_specs=[rows, rows, res((D_MODEL, D_FF)), res((D_MODEL, D_FF)), res((D_FF, D_MODEL)),
                  modspec, modspec, cst((1, D_MODEL)), cst((1, D_MODEL))],
        out_specs=[rows, rows],
        out_shape=[
            jax.ShapeDtypeStruct((B, S, D_MODEL), F32),
            jax.ShapeDtypeStruct((B, S, D_MODEL), BF16),
        ],
        compiler_params=_cparams("parallel", "parallel"),
        name="ffn",
    )(x1, h, wg, wu, wd, mod, modn, ln_w, ln_b)


def _router_kernel(h_ref, rw_ref, rb_ref, ltri_ref, rec_ref, cnt_ref, carry_ref):
    first = (pl.program_id(0) == 0) & (pl.program_id(1) == 0)

    @pl.when(first)
    def _init():
        carry_ref[...] = jnp.zeros_like(carry_ref)

    h = h_ref[0]
    tm = h.shape[0]
    logits = jnp.dot(h, rw_ref[...], preferred_element_type=F32, precision=_HI) + rb_ref[...]
    lane = lax.broadcasted_iota(jnp.int32, (tm, LANES), 1)
    m1 = jnp.max(logits, -1, keepdims=True)
    i1 = jnp.min(jnp.where(logits == m1, lane, LANES), -1, keepdims=True)
    rest = jnp.where(lane == i1, -jnp.inf, logits)
    m2 = jnp.max(rest, -1, keepdims=True)
    i2 = jnp.min(jnp.where(rest == m2, lane, LANES), -1, keepdims=True)
    e21 = jnp.exp(m2 - m1)
    g1 = 1.0 / (1.0 + e21)
    g2 = e21 / (1.0 + e21)
    oh1 = lane == i1
    oh2 = lane == i2
    cnt = oh1.astype(F32) + oh2.astype(F32)
    before = jnp.dot(ltri_ref[...], cnt.astype(BF16), preferred_element_type=F32) + carry_ref[0:1, :]
    rank1 = jnp.sum(jnp.where(oh1, before, 0.0), -1, keepdims=True)
    rank2 = jnp.sum(jnp.where(oh2, before, 0.0), -1, keepdims=True)
    carry_ref[0:1, :] = carry_ref[0:1, :] + jnp.sum(cnt, 0, keepdims=True)
    cols = (i1.astype(F32), i2.astype(F32), g1, g2, rank1, rank2)
    rec = jnp.zeros((tm, LANES), F32)
    for c, val in enumerate(cols):
        rec = jnp.where(lane == c, val, rec)
    rec_ref[0] = rec
    cnt_ref[...] = carry_ref[...]


def _router(h, rw_pad, rb_pad, ltri):
    B, S, _ = h.shape
    tm = min(TM_ROUTE, S)
    return pl.pallas_call(
        _router_kernel,
        grid=(B, S // tm),
        in_specs=[
            pl.BlockSpec((1, tm, D_MODEL), lambda b, i: (b, i, 0)),
            pl.BlockSpec((D_MODEL, LANES), lambda b, i: (0, 0)),
            pl.BlockSpec((1, LANES), lambda b, i: (0, 0)),
            pl.BlockSpec((tm, tm), lambda b, i: (0, 0)),
        ],
        out_specs=[
            pl.BlockSpec((1, tm, LANES), lambda b, i: (b, i, 0)),
            pl.BlockSpec((8, LANES), lambda b, i: (0, 0)),
        ],
        out_shape=[
            jax.ShapeDtypeStruct((B, S, LANES), F32),
            jax.ShapeDtypeStruct((8, LANES), F32),
        ],
        scratch_shapes=[pltpu.VMEM((8, LANES), F32)],
        compiler_params=_cparams("arbitrary", "arbitrary"),
        name="router",
    )(h, rw_pad, rb_pad, ltri)


def _row_copy(src_hbm, dst_hbm, src_row, dst_row, sem):
    return pltpu.make_async_copy(src_hbm.at[pl.ds(src_row, 1)], dst_hbm.at[pl.ds(dst_row, 1)], sem)


def _scatter_kernel(dest_ref, h_hbm, xs_in_hbm, xs_hbm, sem, *, tm):
    del xs_in_hbm
    base = pl.program_id(0) * tm

    def start(r, _):
        t = base + r
        _row_copy(h_hbm, xs_hbm, t, dest_ref[2 * t], sem).start()
        _row_copy(h_hbm, xs_hbm, t, dest_ref[2 * t + 1], sem).start()
        return 0

    lax.fori_loop(0, tm, start, 0)

    def wait(r, _):
        _row_copy(h_hbm, xs_hbm, 0, 0, sem).wait()
        _row_copy(h_hbm, xs_hbm, 0, 0, sem).wait()
        return 0

    lax.fori_loop(0, tm, wait, 0)


def _scatter(dest_flat, h2d, m_pad):
    n = h2d.shape[0]
    tm = min(TM_SCAT, n)
    xs0 = jnp.zeros((m_pad, D_MODEL), h2d.dtype)
    return pl.pallas_call(
        functools.partial(_scatter_kernel, tm=tm),
        grid_spec=pltpu.PrefetchScalarGridSpec(
            num_scalar_prefetch=1,
            grid=(n // tm,),
            in_specs=[pl.BlockSpec(memory_space=pl.ANY), pl.BlockSpec(memory_space=pl.ANY)],
            out_specs=pl.BlockSpec(memory_space=pl.ANY),
            scratch_shapes=[pltpu.SemaphoreType.DMA(())],
        ),
        out_shape=jax.ShapeDtypeStruct((m_pad, D_MODEL), h2d.dtype),
        input_output_aliases={2: 0},
        compiler_params=_cparams("arbitrary"),
        name="scatter",
    )(dest_flat, h2d, xs0)


def _expert_kernel(be_ref, nu_ref, xs_ref, wg_ref, wu_ref, wd_ref, ys_ref, acc_ref):
    i = pl.program_id(0)
    j = pl.program_id(1)
    nj = pl.num_programs(1)
    used = i < nu_ref[0]

    @pl.when(used)
    def _compute():
        x = xs_ref[...].astype(BF16)
        gt = jnp.dot(x, wg_ref[0], preferred_element_type=F32)
        up = jnp.dot(x, wu_ref[0], preferred_element_type=F32)
        act = (gt * jax.nn.sigmoid(gt) * up).astype(BF16)
        t = jnp.dot(act, wd_ref[0], preferred_element_type=F32)

        @pl.when(j == 0)
        def _():
            acc_ref[...] = t

        @pl.when(j > 0)
        def _():
            acc_ref[...] += t

        @pl.when(j == nj - 1)
        def _():
            ys_ref[...] = acc_ref[...]

    @pl.when(jnp.logical_not(used) & (j == nj - 1))
    def _unused():
        ys_ref[...] = jnp.zeros_like(ys_ref)


def _experts(block_e, n_used, xs, wg, wu, wd):
    m_pad = xs.shape[0]
    tm = TM_EXP
    tf = TF_EXP
    nj = D_FF_EXPERT // tf

    def jj(i, j, nu):
        return jnp.where(i < nu[0], j, nj - 1)

    return pl.pallas_call(
        _expert_kernel,
        grid_spec=pltpu.PrefetchScalarGridSpec(
            num_scalar_prefetch=2,
            grid=(m_pad // tm, nj),
            in_specs=[
                pl.BlockSpec((tm, D_MODEL), lambda i, j, be, nu: (i, 0)),
                pl.BlockSpec((1, D_MODEL, tf), lambda i, j, be, nu: (be[i], 0, jj(i, j, nu))),
                pl.BlockSpec((1, D_MODEL, tf), lambda i, j, be, nu: (be[i], 0, jj(i, j, nu))),
                pl.BlockSpec((1, tf, D_MODEL), lambda i, j, be, nu: (be[i], jj(i, j, nu), 0)),
            ],
            out_specs=pl.BlockSpec((tm, D_MODEL), lambda i, j, be, nu: (i, 0)),
            scratch_shapes=[pltpu.VMEM((tm, D_MODEL), F32)],
        ),
        out_shape=jax.ShapeDtypeStruct((m_pad, D_MODEL), F32),
        compiler_params=_cparams("parallel", "arbitrary"),
        name="experts",
    )(block_e, n_used, xs, wg, wu, wd)


def _combine_kernel(dest_ref, x1_ref, gate_ref, ys_hbm, mod_ref, modn_ref, lnw_ref, lnb_ref,
                    x2_ref, u_ref, buf_ref, sem, *, tm, s_len):
    base = pl.program_id(0) * s_len + pl.program_id(1) * tm

    def row(r, k):
        t = base + r
        return pltpu.make_async_copy(ys_hbm.at[pl.ds(dest_ref[2 * t + k], 1)],
                                     buf_ref.at[k, pl.ds(r, 1)], sem)

    def start(r, _):
        row(r, 0).start()
        row(r, 1).start()
        return 0

    lax.fori_loop(0, tm, start, 0)

    def wait(r, _):
        row(r, 0).wait()
        row(r, 1).wait()
        return 0

    lax.fori_loop(0, tm, wait, 0)
    gate = gate_ref[0]
    f = gate[:, 0:1] * buf_ref[0] + gate[:, 1:2] * buf_ref[1]
    _write_epilogue(x1_ref[0], f, mod_ref[0], modn_ref, lnw_ref, lnb_ref, x2_ref, u_ref)


def _combine(dest_flat, x1, gates, ys, mod, modn, ln_w, ln_b):
    B, S, _ = x1.shape
    tm = min(TM_COMB, S)
    rows = pl.BlockSpec((1, tm, D_MODEL), lambda b, i, d: (b, i, 0))
    modspec = pl.BlockSpec((1, 6, D_MODEL), lambda b, i, d: (b, 0, 0))
    cst = pl.BlockSpec((1, D_MODEL), lambda b, i, d: (0, 0))
    return pl.pallas_call(
        functools.partial(_combine_kernel, tm=tm, s_len=S),
        grid_spec=pltpu.PrefetchScalarGridSpec(
            num_scalar_prefetch=1,
            grid=(B, S // tm),
            in_specs=[rows, pl.BlockSpec((1, tm, TOP_K), lambda b, i, d: (b, i, 0)),
                      pl.BlockSpec(memory_space=pl.ANY), modspec, modspec, cst, cst],
            out_specs=[rows, rows],
            scratch_shapes=[pltpu.VMEM((TOP_K, tm, D_MODEL), F32), pltpu.SemaphoreType.DMA(())],
        ),
        out_shape=[
            jax.ShapeDtypeStruct((B, S, D_MODEL), F32),
            jax.ShapeDtypeStruct((B, S, D_MODEL), BF16),
        ],
        compiler_params=_cparams("arbitrary", "arbitrary"),
        name="combine",
    )(dest_flat, x1, gates, ys, mod, modn, ln_w, ln_b)


def _moe(x1, h, rw_pad, rb_pad, ltri, wg, wu, wd, mod, modn, ln_w, ln_b):
    B, S, _ = x1.shape
    n = B * S
    m = n * TOP_K
    rec, cnt = _router(h, rw_pad, rb_pad, ltri)
    rec = rec.reshape(n, LANES)
    e = rec[:, 0:TOP_K].astype(jnp.int32)
    gates = rec[:, TOP_K:2 * TOP_K]
    rank = rec[:, 2 * TOP_K:3 * TOP_K].astype(jnp.int32)
    counts = cnt[0, :N_EXPERTS].astype(jnp.int32)
    padded = (counts + TM_EXP - 1) // TM_EXP * TM_EXP
    pad_end = jnp.cumsum(padded)
    pad_start = pad_end - padded
    dest = (pad_start[e] + rank).reshape(m)
    n_blocks = m // TM_EXP + N_EXPERTS
    block_e = jnp.minimum(
        jnp.searchsorted(pad_end, jnp.arange(n_blocks, dtype=jnp.int32) * TM_EXP, side="right"),
        N_EXPERTS - 1).astype(jnp.int32)
    n_used = (pad_end[-1:] // TM_EXP).astype(jnp.int32)
    xs = _scatter(dest, h.reshape(n, D_MODEL), n_blocks * TM_EXP)
    ys = _experts(block_e, n_used, xs, wg, wu, wd)
    return _combine(dest, x1, gates.reshape(B, S, TOP_K), ys, mod, modn, ln_w, ln_b)


def _mla_weights(w_uq, w_ukv):
    L = w_uq.shape[0]
    H = MLA_HEADS
    half = QK_ROPE // 2
    wq = w_uq.reshape(L, Q_LORA, H, QK_NOPE + QK_ROPE)
    nope, rope = wq[..., :QK_NOPE], wq[..., QK_NOPE:]
    zpad = jnp.zeros((L, Q_LORA, H, HEAD_PAD - QK_NOPE - QK_ROPE), w_uq.dtype)
    wq_pad = jnp.concatenate([nope, rope, zpad], -1).reshape(L, Q_LORA, H * HEAD_PAD)
    rope_rot = jnp.concatenate([-rope[..., half:], rope[..., :half]], -1)
    wq_rot = jnp.concatenate([jnp.zeros_like(nope), rope_rot, zpad], -1).reshape(L, Q_LORA, H * HEAD_PAD)
    wkv = w_ukv.reshape(L, KV_LORA, H, QK_NOPE + V_HEAD)
    k_nope, v = wkv[..., :QK_NOPE], wkv[..., QK_NOPE:]
    wk_pad = jnp.concatenate(
        [k_nope, jnp.zeros((L, KV_LORA, H, HEAD_PAD - QK_NOPE), w_ukv.dtype)], -1
    ).reshape(L, KV_LORA, H * HEAD_PAD)
    wv = v.reshape(L, KV_LORA, MLA_DIM)
    return wq_pad.astype(BF16), wq_rot.astype(BF16), wk_pad.astype(BF16), wv.astype(BF16)


def _rope_placement():
    half = QK_ROPE // 2
    src = jnp.arange(QK_ROPE)[:, None]
    dst = jnp.arange(HEAD_PAD)[None, :]
    place = (dst == src + QK_NOPE).astype(BF16)
    rot = jnp.where(dst == src + QK_NOPE + half, 1.0, 0.0) - jnp.where(dst == src + QK_NOPE - half, 1.0, 0.0)
    rot = jnp.where((src < half) & (dst == src + QK_NOPE + half), 1.0,
                    jnp.where((src >= half) & (dst == src + QK_NOPE - half), -1.0, 0.0))
    return place, rot.astype(BF16)


def kernel(x, c, positions, w_ada, b_ada, w_in, shift_mu, decay_w0, decay_up, aaa_a0, aaa_up, gate_up, k_k, k_a, r_k, gn_w, gn_b, q_norm_w, w_uq, kv_norm_w, w_ukv, p_a, p_b, w_o, ln1_w, ln1_b, ln2_w, ln2_b, ffn_w_gate, ffn_w_up, ffn_w_down, router_w, router_b, moe_w_gate, moe_w_up, moe_w_down):
    B, S, _ = x.shape
    n = B * S
    L = DEPTH
    assert S % RW_CHUNK == 0 and S % LANES == 0

    mod = _ada(c, w_ada, b_ada).reshape(L, B, 6, D_MODEL)
    posf = positions.astype(F32).reshape(B, S, 1)

    w_shift = w_in[:, :, :SHIFT_DIM].astype(BF16)
    w_lat = jnp.pad(w_in[:, :, SHIFT_DIM:SHIFT_DIM + LAT_DIM],
                    ((0, 0), (0, 0), (0, LAT_PAD - LAT_DIM))).astype(BF16)
    w_gates = w_in[:, :, SHIFT_DIM + LAT_DIM:].astype(BF16)
    wq_pad, wq_rot, wk_pad, wv = _mla_weights(w_uq, w_ukv)
    place, placer = _rope_placement()
    lane_head = jnp.arange(RW_DIM) // RW_HEAD
    hsum = (lane_head[:, None] == lane_head[None, :]).astype(BF16)
    row1 = lambda a, l: a[l].reshape(1, -1)
    tm_r = min(TM_ROUTE, S)
    ltri = (jnp.arange(tm_r)[None, :] < jnp.arange(tm_r)[:, None]).astype(BF16)
    rw_pad = jnp.pad(router_w, ((0, 0), (0, 0), (0, LANES - N_EXPERTS)))
    rb_pad = jnp.pad(router_b, ((0, 0), (0, LANES - N_EXPERTS)), constant_values=NEG_BIG)

    u = _lnmod(x, mod[0])
    for l in range(L):
        u2 = u.reshape(n, D_MODEL)
        ps = _mm(u2, w_shift[l], F32).reshape(B, S, SHIFT_DIM)
        lat = _mm(u2, w_lat[l], F32).reshape(B, S, LAT_PAD)
        sg = _mm(u2, w_gates[l], BF16, act="sigmoid").reshape(B, S, 2 * D_MODEL)
        ya = _rwkv(ps, row1(shift_mu, l), row1(decay_w0, l), decay_up[l].astype(BF16), row1(aaa_a0, l),
                   aaa_up[l].astype(BF16), gate_up[l].astype(BF16), row1(k_k, l), row1(k_a, l),
                   row1(r_k, l), row1(gn_w, l), row1(gn_b, l), hsum)
        q, k, v = _mla_prep(lat, posf, row1(q_norm_w, l), wq_pad[l], wq_rot[l], row1(kv_norm_w, l),
                            wk_pad[l], wv[l], place, placer)
        yb = _attn(q, k, v)
        moe_layer = l % 2 == 1
        x1, h = _merge_out(x, ya, yb, sg, p_a[l].astype(BF16), p_b[l].astype(BF16), w_o[l].astype(BF16),
                           mod[l], row1(ln1_w, l), row1(ln1_b, l), F32 if moe_layer else BF16)
        modn = mod[min(l + 1, L - 1)]
        if moe_layer:
            x, u = _moe(x1, h, rw_pad[l // 2], rb_pad[l // 2].reshape(1, LANES), ltri,
                        moe_w_gate[l // 2].astype(BF16), moe_w_up[l // 2].astype(BF16),
                        moe_w_down[l // 2].astype(BF16), mod[l], modn, row1(ln2_w, l), row1(ln2_b, l))
        else:
            x, u = _ffn(x1, h, ffn_w_gate[l // 2].astype(BF16), ffn_w_up[l // 2].astype(BF16),
                        ffn_w_down[l // 2].astype(BF16), mod[l], modn, row1(ln2_w, l), row1(ln2_b, l))
    return x
```

```python
import functools
import math

import jax
import jax.numpy as jnp
from jax import lax
from jax.experimental import pallas as pl
from jax.experimental.pallas import tpu as pltpu

F32 = jnp.float32
BF16 = jnp.bfloat16

D_MODEL = 1024
DEPTH = 4
RW_HEADS = 8
RW_HEAD = 64
RW_DIM = RW_HEADS * RW_HEAD
DECAY_LORA = 64
AAA_LORA = 64
GATE_LORA = 128
GN_EPS = 64e-5
MLA_HEADS = 8
QK_NOPE = 64
QK_ROPE = 32
V_HEAD = 64
Q_LORA = 384
KV_LORA = 256
MLA_DIM = MLA_HEADS * V_HEAD
ROPE_THETA = 10000.0
SHIFT_DIM = 3 * RW_DIM + DECAY_LORA + AAA_LORA + GATE_LORA
LAT_DIM = Q_LORA + KV_LORA + QK_ROPE
LAT_PAD = 768
D_FF = 2816
N_EXPERTS = 8
TOP_K = 2
D_FF_EXPERT = 3584
ALPHA = (2.0 * DEPTH) ** 0.25
LN_EPS = 1e-5
RMS_EPS = 1e-6
LANES = 128
HEAD_PAD = 128
NEG_BIG = -1e30

RW_CHUNK = 64
RW_SEQS = 2
TM_MM = 512
TM_PREP = 256
TQ = 512
TM_MERGE = 256
TM_FFN = 512
FF_CHUNK = 256
TM_ROUTE = 512
TM_EXP = 512
TF_EXP = 512
TM_COMB = 256

_HI = lax.Precision.HIGHEST


def _cparams(*sem):
    return pltpu.CompilerParams(dimension_semantics=sem)


def _dot(a, b):
    return jnp.dot(a.astype(BF16), b.astype(BF16), preferred_element_type=F32)


def _dot_nt(a, b):
    return lax.dot_general(a.astype(BF16), b.astype(BF16), (((1,), (1,)), ((), ())),
                           preferred_element_type=F32)


def _dot_tn(a, b):
    return lax.dot_general(a.astype(BF16), b.astype(BF16), (((0,), (0,)), ((), ())),
                           preferred_element_type=F32)


def _split(x, n):
    parts = []
    for _ in range(n - 1):
        p = x.astype(BF16)
        parts.append(p)
        x = x - p.astype(F32)
    parts.append(x.astype(BF16))
    return parts


def _dot_split_lhs(x, w_bf16, n):
    acc = None
    for p in _split(x, n):
        t = jnp.dot(p, w_bf16, preferred_element_type=F32)
        acc = t if acc is None else acc + t
    return acc


def _dot_split_rhs(w_bf16, x, n):
    acc = None
    for p in _split(x, n):
        t = jnp.dot(w_bf16, p, preferred_element_type=F32)
        acc = t if acc is None else acc + t
    return acc


def _layer_norm(x):
    mu = jnp.mean(x, -1, keepdims=True)
    xc = x - mu
    var = jnp.mean(xc * xc, -1, keepdims=True)
    return xc * lax.rsqrt(var + LN_EPS)


def _full(shape):
    nd = len(shape)
    return pl.BlockSpec(shape, lambda *_: (0,) * nd)


def _ada_kernel(c_ref, w_ref, b_ref, o_ref):
    c = c_ref[...]
    ca = c * jax.nn.sigmoid(c)
    o_ref[0] = jnp.dot(ca, w_ref[0], preferred_element_type=F32, precision=_HI) + b_ref[0]


def _ada(c, w_ada, b_ada):
    L, _, six_d = w_ada.shape
    B = c.shape[0]
    nj = six_d // D_MODEL
    return pl.pallas_call(
        _ada_kernel,
        grid=(L, nj),
        in_specs=[
            pl.BlockSpec((B, D_MODEL), lambda l, j: (0, 0)),
            pl.BlockSpec((1, D_MODEL, D_MODEL), lambda l, j: (l, 0, j)),
            pl.BlockSpec((1, 1, D_MODEL), lambda l, j: (l, 0, j)),
        ],
        out_specs=pl.BlockSpec((1, B, D_MODEL), lambda l, j: (l, 0, j)),
        out_shape=jax.ShapeDtypeStruct((L, B, six_d), F32),
        compiler_params=_cparams("parallel", "parallel"),
        name="ada",
    )(c, w_ada, b_ada.reshape(L, 1, six_d))


def _lnmod_kernel(x_ref, mod_ref, o_ref):
    m = mod_ref[0]
    o_ref[0] = (_layer_norm(x_ref[0]) * (1.0 + m[1:2]) + m[0:1]).astype(o_ref.dtype)


def _lnmod(x, mod):
    B, S, _ = x.shape
    tm = min(TM_MM, S)
    return pl.pallas_call(
        _lnmod_kernel,
        grid=(B, S // tm),
        in_specs=[
            pl.BlockSpec((1, tm, D_MODEL), lambda b, i: (b, i, 0)),
            pl.BlockSpec((1, 6, D_MODEL), lambda b, i: (b, 0, 0)),
        ],
        out_specs=pl.BlockSpec((1, tm, D_MODEL), lambda b, i: (b, i, 0)),
        out_shape=jax.ShapeDtypeStruct((B, S, D_MODEL), BF16),
        compiler_params=_cparams("parallel", "parallel"),
        name="lnmod",
    )(x, mod)


def _mm_kernel(a_ref, w_ref, o_ref, *, act):
    acc = jnp.dot(a_ref[...], w_ref[...], preferred_element_type=F32)
    if act == "sigmoid":
        acc = jax.nn.sigmoid(acc)
    o_ref[...] = acc.astype(o_ref.dtype)


def _mm(a, w, out_dtype, act=None):
    n, k = a.shape
    n_out = w.shape[1]
    tm = min(TM_MM, n)
    return pl.pallas_call(
        functools.partial(_mm_kernel, act=act),
        grid=(n // tm,),
        in_specs=[
            pl.BlockSpec((tm, k), lambda i: (i, 0)),
            pl.BlockSpec((k, n_out), lambda i: (0, 0)),
        ],
        out_specs=pl.BlockSpec((tm, n_out), lambda i: (i, 0)),
        out_shape=jax.ShapeDtypeStruct((n, n_out), out_dtype),
        compiler_params=_cparams("parallel"),
        name="mm",
    )(a, w)


def _rwkv_kernel(ps_ref, mu_ref, w0_ref, dup_ref, a0_ref, aup_ref, gup_ref, kk_ref, ka_ref, rk_ref,
                 gnw_ref, gnb_ref, hsum_ref, o_ref, state_ref, prev_ref, y_ref):
    C = RW_CHUNK
    N = RW_HEAD
    G = ps_ref.shape[0]

    @pl.when(pl.program_id(1) == 0)
    def _init():
        state_ref[...] = jnp.zeros_like(state_ref)
        prev_ref[...] = jnp.zeros_like(prev_ref)

    hsum = hsum_ref[...]
    ri = lax.broadcasted_iota(jnp.int32, (C, C), 0)
    ci = lax.broadcasted_iota(jnp.int32, (C, C), 1)
    tri = (ci <= ri).astype(BF16)
    r4 = lax.broadcasted_iota(jnp.int32, (2 * C, 4 * C), 0)
    c4 = lax.broadcasted_iota(jnp.int32, (2 * C, 4 * C), 1)
    keep = (c4 % C) < (r4 % C) + (r4 >= C).astype(jnp.int32)
    r1 = lax.broadcasted_iota(jnp.int32, (C, 2 * C), 0)
    c1 = lax.broadcasted_iota(jnp.int32, (C, 2 * C), 1)
    low = c1 < C
    eye_hi = (c1 == r1 + C).astype(F32)

    prep = []
    for s in range(G):
        p = ps_ref[s]
        row = lax.broadcasted_iota(jnp.int32, p.shape, 0)
        shifted = jnp.where(row == 0, prev_ref[s, 0:1, :], pltpu.roll(p, 1, 0))
        prev_ref[s, 0:1, :] = p[C - 1:C, :]
        xs = p + (shifted - p) * mu_ref[...]
        r = xs[:, 0:RW_DIM]
        k = xs[:, RW_DIM:2 * RW_DIM]
        v = xs[:, 2 * RW_DIM:3 * RW_DIM]
        o = 3 * RW_DIM
        dw = xs[:, o:o + DECAY_LORA]
        da = xs[:, o + DECAY_LORA:o + DECAY_LORA + AAA_LORA]
        dg = xs[:, o + DECAY_LORA + AAA_LORA:SHIFT_DIM]
        lw = -math.exp(-0.5) * jax.nn.sigmoid(w0_ref[...] + _dot(jnp.tanh(dw), dup_ref[...]))
        a = jax.nn.sigmoid(a0_ref[...] + _dot(da, aup_ref[...]))
        gate = _dot(jax.nn.sigmoid(dg), gup_ref[...])
        kk = k * kk_ref[...]
        kk = kk * lax.rsqrt(jnp.maximum(_dot_split_lhs(kk * kk, hsum, 2), 1e-24))
        k_mod = k * (1.0 + (a - 1.0) * ka_ref[...])
        b = kk * a
        bonus = _dot_split_lhs(r * k_mod * rk_ref[...], hsum, 2) * v
        cw = _dot_split_rhs(tri, lw, 3)
        cw_last = cw[C - 1:C, :]
        e_neg = jnp.exp(-cw)
        e_rem = jnp.exp(cw_last - cw)
        prep.append(dict(
            kp=(kk * jnp.exp(cw - lw)).astype(BF16), rp=(r * jnp.exp(cw)).astype(BF16),
            bm=(b * e_neg).astype(BF16), km=(k_mod * e_neg).astype(BF16),
            bt=(b * e_rem).astype(BF16), kt=(k_mod * e_rem).astype(BF16),
            v=v.astype(BF16), w_c=jnp.exp(cw_last), bonus=bonus, gate=gate))

    chains = [(s, h) for s in range(G) for h in range(RW_HEADS)]
    sl = lambda h: slice(h * N, (h + 1) * N)

    krs, aas = [], []
    for s, h in chains:
        d = prep[s]
        kr = jnp.concatenate([d["kp"][:, sl(h)], d["rp"][:, sl(h)]], axis=0)
        bk = jnp.concatenate([d["bm"][:, sl(h)], d["km"][:, sl(h)], d["km"][:, sl(h)], d["bm"][:, sl(h)]],
                             axis=0)
        krs.append(kr)
        aas.append(jnp.where(keep, _dot_nt(kr, bk), 0.0))
    s0s = [state_ref[s, h] for s, h in chains]
    xxs = [_dot_nt(kr, s0) for kr, s0 in zip(krs, s0s)]
    rts = [jnp.where(low, _dot(aa[0:C, 0:C], aa[0:C, 0:2 * C]), eye_hi - aa[0:C, 2 * C:4 * C]) for aa in aas]
    for _ in range(int(math.log2(C)) - 1):
        outs = [_dot(rt[:, 0:C], rt) for rt in rts]
        rts = [jnp.where(low, out, rt + out) for rt, out in zip(rts, outs)]
    vs = [prep[s]["v"][:, sl(h)] for s, h in chains]
    gms = [xx[0:C] + _dot(aa[0:C, 2 * C:3 * C], v_h) for xx, aa, v_h in zip(xxs, aas, vs)]
    us = [-_dot(rt[:, C:2 * C], gm) for rt, gm in zip(rts, gms)]
    uvs = [jnp.concatenate([u.astype(BF16), v_h], axis=0) for u, v_h in zip(us, vs)]
    ys = [xx[C:2 * C] + _dot(aa[C:2 * C, 0:2 * C], uv) for xx, aa, uv in zip(xxs, aas, uvs)]
    news = []
    for (s, h), s0, uv in zip(chains, s0s, uvs):
        d = prep[s]
        btkt = jnp.concatenate([d["bt"][:, sl(h)], d["kt"][:, sl(h)]], axis=0)
        news.append(s0 * d["w_c"][:, sl(h)] + _dot_tn(uv, btkt))
    for (s, h), y_h, new in zip(chains, ys, news):
        y_ref[s, :, sl(h)] = y_h
        state_ref[s, h] = new

    for s in range(G):
        d = prep[s]
        y = y_ref[s]
        mean = _dot_split_lhs(y, hsum, 2) * (1.0 / N)
        yc = y - mean
        var = _dot_split_lhs(yc * yc, hsum, 2) * (1.0 / N)
        yn = yc * lax.rsqrt(var + GN_EPS) * gnw_ref[...] + gnb_ref[...]
        o_ref[s] = ((yn + d["bonus"]) * d["gate"]).astype(o_ref.dtype)


def _rwkv(ps, mu, w0, dup, a0, aup, gup, k_k, k_a, r_k, gn_w, gn_b, hsum):
    B, S, _ = ps.shape
    C = RW_CHUNK
    G = RW_SEQS if B % RW_SEQS == 0 else 1
    row = lambda n: pl.BlockSpec((1, n), lambda b, j: (0, 0))
    return pl.pallas_call(
        _rwkv_kernel,
        grid=(B // G, S // C),
        in_specs=[
            pl.BlockSpec((G, C, SHIFT_DIM), lambda b, j: (b, j, 0)),
            row(SHIFT_DIM), row(RW_DIM),
            pl.BlockSpec((DECAY_LORA, RW_DIM), lambda b, j: (0, 0)),
            row(RW_DIM),
            pl.BlockSpec((AAA_LORA, RW_DIM), lambda b, j: (0, 0)),
            pl.BlockSpec((GATE_LORA, RW_DIM), lambda b, j: (0, 0)),
            row(RW_DIM), row(RW_DIM), row(RW_DIM), row(RW_DIM), row(RW_DIM),
            pl.BlockSpec((RW_DIM, RW_DIM), lambda b, j: (0, 0)),
        ],
        out_specs=pl.BlockSpec((G, C, RW_DIM), lambda b, j: (b, j, 0)),
        out_shape=jax.ShapeDtypeStruct((B, S, RW_DIM), BF16),
        scratch_shapes=[
            pltpu.VMEM((G, RW_HEADS, RW_HEAD, RW_HEAD), F32),
            pltpu.VMEM((G, 8, SHIFT_DIM), F32),
            pltpu.VMEM((G, C, RW_DIM), F32),
        ],
        compiler_params=_cparams("parallel", "arbitrary"),
        name="rwkv",
    )(ps, mu, w0, dup, a0, aup, gup, k_k, k_a, r_k, gn_w, gn_b, hsum)


def _mla_prep_kernel(lat_ref, pos_ref, qnw_ref, wq_ref, wqr_ref, kvnw_ref, wk_ref, wv_ref,
                     place_ref, placer_ref, q_ref, k_ref, v_ref):
    lat = lat_ref[0]
    c_q = lat[:, 0:Q_LORA]
    c_kv = lat[:, Q_LORA:Q_LORA + KV_LORA]
    k_rope = lat[:, Q_LORA + KV_LORA:LAT_DIM]

    def rms(z, w):
        return z * lax.rsqrt(jnp.mean(z * z, -1, keepdims=True) + RMS_EPS) * w

    half = QK_ROPE // 2
    lane = lax.broadcasted_iota(jnp.int32, (1, HEAD_PAD), 1)
    in_rope = (lane >= QK_NOPE) & (lane < QK_NOPE + QK_ROPE)
    fidx = jnp.where(lane < QK_NOPE + half, lane - QK_NOPE, lane - QK_NOPE - half)
    inv_freq = jnp.exp(-math.log(ROPE_THETA) * fidx.astype(F32) / half)
    ang = pos_ref[0] * inv_freq
    cos_r = jnp.where(in_rope, jnp.cos(ang), 0.0)
    sin_r = jnp.where(in_rope, jnp.sin(ang), 0.0)
    scale = (QK_NOPE + QK_ROPE) ** -0.5
    q_mul = jnp.where(lane < QK_NOPE, 1.0, cos_r) * scale
    q_mul_rot = sin_r * scale

    cq_n = rms(c_q, qnw_ref[...]).astype(BF16)
    q_all = jnp.dot(cq_n, wq_ref[...], preferred_element_type=F32)
    q_rot = jnp.dot(cq_n, wqr_ref[...], preferred_element_type=F32)
    ckv_n = rms(c_kv, kvnw_ref[...]).astype(BF16)
    k_all = jnp.dot(ckv_n, wk_ref[...], preferred_element_type=F32)
    v_all = jnp.dot(ckv_n, wv_ref[...], preferred_element_type=F32)
    kr_pad = (_dot_split_lhs(k_rope, place_ref[...], 3) * cos_r
              + _dot_split_lhs(k_rope, placer_ref[...], 3) * sin_r)
    for h in range(MLA_HEADS):
        sl = slice(h * HEAD_PAD, (h + 1) * HEAD_PAD)
        q_ref[0, h] = (q_all[:, sl] * q_mul + q_rot[:, sl] * q_mul_rot).astype(q_ref.dtype)
        k_ref[0, h] = (k_all[:, sl] + kr_pad).astype(k_ref.dtype)
    for hp in range(MLA_HEADS // 2):
        v_ref[0, hp] = v_all[:, hp * LANES:(hp + 1) * LANES].astype(v_ref.dtype)


def _mla_prep(lat, posf, qnw, wq, wqr, kvnw, wk, wv, place, placer):
    B, S, _ = lat.shape
    tm = min(TM_PREP, S)
    H = MLA_HEADS
    cst = lambda shape: pl.BlockSpec(shape, lambda b, i: (0,) * len(shape))
    return pl.pallas_call(
        _mla_prep_kernel,
        grid=(B, S // tm),
        in_specs=[
            pl.BlockSpec((1, tm, LAT_PAD), lambda b, i: (b, i, 0)),
            pl.BlockSpec((1, tm, 1), lambda b, i: (b, i, 0)),
            cst((1, Q_LORA)), cst((Q_LORA, H * HEAD_PAD)), cst((Q_LORA, H * HEAD_PAD)),
            cst((1, KV_LORA)), cst((KV_LORA, H * HEAD_PAD)), cst((KV_LORA, MLA_DIM)),
            cst((QK_ROPE, HEAD_PAD)), cst((QK_ROPE, HEAD_PAD)),
        ],
        out_specs=[
            pl.BlockSpec((1, H, tm, HEAD_PAD), lambda b, i: (b, 0, i, 0)),
            pl.BlockSpec((1, H, tm, HEAD_PAD), lambda b, i: (b, 0, i, 0)),
            pl.BlockSpec((1, H // 2, tm, LANES), lambda b, i: (b, 0, i, 0)),
        ],
        out_shape=[
            jax.ShapeDtypeStruct((B, H, S, HEAD_PAD), BF16),
            jax.ShapeDtypeStruct((B, H, S, HEAD_PAD), BF16),
            jax.ShapeDtypeStruct((B, H // 2, S, LANES), BF16),
        ],
        compiler_params=_cparams("parallel", "parallel"),
        name="mla_prep",
    )(lat, posf, qnw, wq, wqr, kvnw, wk, wv, place, placer)


def _attn_kernel(q_ref, k_ref, v_ref, o_ref, *, tq):
    i = pl.program_id(2)
    lane = lax.broadcasted_iota(jnp.int32, (tq, LANES), 1)
    ri = lax.broadcasted_iota(jnp.int32, (tq, tq), 0)
    ci = lax.broadcasted_iota(jnp.int32, (tq, tq), 1)
    qs = [q_ref[0, hh] for hh in range(2)]

    def step(j, carry, masked):
        start = pl.multiple_of(j * tq, tq)
        vs = v_ref[0, 0, pl.ds(start, tq), :]
        ss = [lax.dot_general(qs[hh], k_ref[0, hh, pl.ds(start, tq), :], (((1,), (1,)), ((), ())),
                              preferred_element_type=F32) for hh in range(2)]
        new = []
        ps = []
        for hh in range(2):
            m, l, acc = carry[hh]
            s = ss[hh]
            if masked:
                s = jnp.where(ci <= ri, s, NEG_BIG)
            m_new = jnp.maximum(m, jnp.max(s, -1, keepdims=True))
            alpha = jnp.exp(m - m_new)
            p = jnp.exp(s - m_new)
            l = alpha * l + jnp.sum(p, -1, keepdims=True)
            ps.append(p.astype(BF16))
            new.append((m_new, l, alpha * acc))
        pvs = [jnp.dot(ps[hh], vs, preferred_element_type=F32) for hh in range(2)]
        return tuple((new[hh][0], new[hh][1], new[hh][2] + pvs[hh]) for hh in range(2))

    init1 = (jnp.full((tq, 1), NEG_BIG, F32), jnp.zeros((tq, 1), F32), jnp.zeros((tq, LANES), F32))
    carry = lax.fori_loop(0, i, functools.partial(step, masked=False), (init1, init1))
    carry = step(i, carry, True)
    outs = [carry[hh][2] / carry[hh][1] for hh in range(2)]
    o_ref[0] = jnp.where(lane < V_HEAD, outs[0], outs[1]).astype(o_ref.dtype)


def _attn(q, k, v):
    B, H, S, _ = q.shape
    tq = min(TQ, S)
    return pl.pallas_call(
        functools.partial(_attn_kernel, tq=tq),
        grid=(B, H // 2, S // tq),
        in_specs=[
            pl.BlockSpec((1, 2, tq, HEAD_PAD), lambda b, p, i: (b, p, i, 0)),
            pl.BlockSpec((1, 2, S, HEAD_PAD), lambda b, p, i: (b, p, 0, 0)),
            pl.BlockSpec((1, 1, S, LANES), lambda b, p, i: (b, p, 0, 0)),
        ],
        out_specs=pl.BlockSpec((1, tq, LANES), lambda b, p, i: (b, i, p)),
        out_shape=jax.ShapeDtypeStruct((B, S, MLA_DIM), BF16),
        compiler_params=_cparams("parallel", "parallel", "arbitrary"),
        name="attn",
    )(q, k, v)


def _residual_epilogue(x, f, gate, ln_w, ln_b):
    return _layer_norm(ALPHA * x + gate * f) * ln_w + ln_b


def _merge_out_kernel(x_ref, ya_ref, yb_ref, sga_ref, sgb_ref, pa_ref, pb_ref, wo_ref, mod_ref,
                      lnw_ref, lnb_ref, x1_ref, h_ref):
    m = mod_ref[0]
    ma = jnp.dot(ya_ref[0], pa_ref[...], preferred_element_type=F32)
    mb = jnp.dot(yb_ref[0], pb_ref[...], preferred_element_type=F32)
    merged = sga_ref[0].astype(F32) * ma + sgb_ref[0].astype(F32) * mb
    mix = jnp.dot(merged.astype(BF16), wo_ref[...], preferred_element_type=F32)
    x1 = _residual_epilogue(x_ref[0], mix, m[2:3], lnw_ref[...], lnb_ref[...])
    x1_ref[0] = x1
    h_ref[0] = (_layer_norm(x1) * (1.0 + m[4:5]) + m[3:4]).astype(h_ref.dtype)


def _merge_out(x, ya, yb, sg, pa, pb, wo, mod, ln_w, ln_b, h_dtype):
    B, S, _ = x.shape
    tm = min(TM_MERGE, S)
    cst = lambda shape: pl.BlockSpec(shape, lambda b, i: (0,) * len(shape))
    rows = lambda n: pl.BlockSpec((1, tm, n), lambda b, i: (b, i, 0))
    return pl.pallas_call(
        _merge_out_kernel,
        grid=(B, S // tm),
        in_specs=[
            rows(D_MODEL), rows(RW_DIM), rows(MLA_DIM),
            pl.BlockSpec((1, tm, D_MODEL), lambda b, i: (b, i, 0)),
            pl.BlockSpec((1, tm, D_MODEL), lambda b, i: (b, i, 1)),
            cst((RW_DIM, D_MODEL)), cst((MLA_DIM, D_MODEL)), cst((D_MODEL, D_MODEL)),
            pl.BlockSpec((1, 6, D_MODEL), lambda b, i: (b, 0, 0)),
            cst((1, D_MODEL)), cst((1, D_MODEL)),
        ],
        out_specs=[rows(D_MODEL), rows(D_MODEL)],
        out_shape=[
            jax.ShapeDtypeStruct((B, S, D_MODEL), F32),
            jax.ShapeDtypeStruct((B, S, D_MODEL), h_dtype),
        ],
        compiler_params=_cparams("parallel", "parallel"),
        name="merge_out",
    )(x, ya, yb, sg, sg, pa, pb, wo, mod, ln_w, ln_b)


def _write_epilogue(x1, f, m, modn_ref, lnw_ref, lnb_ref, x2_ref, u_ref):
    x2 = _residual_epilogue(x1, f, m[5:6], lnw_ref[...], lnb_ref[...])
    x2_ref[0] = x2
    mn = modn_ref[0]
    u_ref[0] = (_layer_norm(x2) * (1.0 + mn[1:2]) + mn[0:1]).astype(u_ref.dtype)


def _ffn_kernel(x1_ref, h_ref, wg_ref, wu_ref, wd_ref, mod_ref, modn_ref, lnw_ref, lnb_ref,
                x2_ref, u_ref):
    h = h_ref[0]
    acc = None
    for j in range(D_FF // FF_CHUNK):
        sl = slice(j * FF_CHUNK, (j + 1) * FF_CHUNK)
        gt = jnp.dot(h, wg_ref[:, sl], preferred_element_type=F32)
        up = jnp.dot(h, wu_ref[:, sl], preferred_element_type=F32)
        act = (gt * jax.nn.sigmoid(gt) * up).astype(BF16)
        t = jnp.dot(act, wd_ref[sl, :], preferred_element_type=F32)
        acc = t if acc is None else acc + t
    _write_epilogue(x1_ref[0], acc, mod_ref[0], modn_ref, lnw_ref, lnb_ref, x2_ref, u_ref)


def _ffn(x1, h, wg, wu, wd, mod, modn, ln_w, ln_b):
    B, S, _ = x1.shape
    tm = min(TM_FFN, S)
    res = lambda shape: pl.BlockSpec(shape, lambda b, i: (0,) * len(shape), pipeline_mode=pl.Buffered(1))
    cst = lambda shape: pl.BlockSpec(shape, lambda b, i: (0,) * len(shape))
    rows = pl.BlockSpec((1, tm, D_MODEL), lambda b, i: (b, i, 0))
    modspec = pl.BlockSpec((1, 6, D_MODEL), lambda b, i: (b, 0, 0))
    return pl.pallas_call(
        _ffn_kernel,
        grid=(B, S // tm),
        in_specs=[rows, rows, res((D_MODEL, D_FF)), res((D_MODEL, D_FF)), res((D_FF, D_MODEL)),
                  modspec, modspec, cst((1, D_MODEL)), cst((1, D_MODEL))],
        out_specs=[rows, rows],
        out_shape=[
            jax.ShapeDtypeStruct((B, S, D_MODEL), F32),
            jax.ShapeDtypeStruct((B, S, D_MODEL), BF16),
        ],
        compiler_params=_cparams("parallel", "parallel"),
        name="ffn",
    )(x1, h, wg, wu, wd, mod, modn, ln_w, ln_b)


def _router_kernel(h_ref, rw_ref, rb_ref, ltri_ref, rec_ref, cnt_ref, carry_ref):
    first = (pl.program_id(0) == 0) & (pl.program_id(1) == 0)

    @pl.when(first)
    def _init():
        carry_ref[...] = jnp.zeros_like(carry_ref)

    h = h_ref[0]
    tm = h.shape[0]
    logits = jnp.dot(h, rw_ref[...], preferred_element_type=F32, precision=_HI) + rb_ref[...]
    lane = lax.broadcasted_iota(jnp.int32, (tm, LANES), 1)
    m1 = jnp.max(logits, -1, keepdims=True)
    i1 = jnp.min(jnp.where(logits == m1, lane, LANES), -1, keepdims=True)
    rest = jnp.where(lane == i1, -jnp.inf, logits)
    m2 = jnp.max(rest, -1, keepdims=True)
    i2 = jnp.min(jnp.where(rest == m2, lane, LANES), -1, keepdims=True)
    e21 = jnp.exp(m2 - m1)
    g1 = 1.0 / (1.0 + e21)
    g2 = e21 / (1.0 + e21)
    oh1 = lane == i1
    oh2 = lane == i2
    cnt = oh1.astype(F32) + oh2.astype(F32)
    before = jnp.dot(ltri_ref[...], cnt.astype(BF16), preferred_element_type=F32) + carry_ref[0:1, :]
    rank1 = jnp.sum(jnp.where(oh1, before, 0.0), -1, keepdims=True)
    rank2 = jnp.sum(jnp.where(oh2, before, 0.0), -1, keepdims=True)
    carry_ref[0:1, :] = carry_ref[0:1, :] + jnp.sum(cnt, 0, keepdims=True)
    cols = (i1.astype(F32), i2.astype(F32), g1, g2, rank1, rank2)
    rec = jnp.zeros((tm, LANES), F32)
    for c, val in enumerate(cols):
        rec = jnp.where(lane == c, val, rec)
    rec_ref[0] = rec
    cnt_ref[...] = carry_ref[...]


def _router(h, rw_pad, rb_pad, ltri):
    B, S, _ = h.shape
    tm = min(TM_ROUTE, S)
    return pl.pallas_call(
        _router_kernel,
        grid=(B, S // tm),
        in_specs=[
            pl.BlockSpec((1, tm, D_MODEL), lambda b, i: (b, i, 0)),
            pl.BlockSpec((D_MODEL, LANES), lambda b, i: (0, 0)),
            pl.BlockSpec((1, LANES), lambda b, i: (0, 0)),
            pl.BlockSpec((tm, tm), lambda b, i: (0, 0)),
        ],
        out_specs=[
            pl.BlockSpec((1, tm, LANES), lambda b, i: (b, i, 0)),
            pl.BlockSpec((8, LANES), lambda b, i: (0, 0)),
        ],
        out_shape=[
            jax.ShapeDtypeStruct((B, S, LANES), F32),
            jax.ShapeDtypeStruct((8, LANES), F32),
        ],
        scratch_shapes=[pltpu.VMEM((8, LANES), F32)],
        compiler_params=_cparams("arbitrary", "arbitrary"),
        name="router",
    )(h, rw_pad, rb_pad, ltri)


def _expert_kernel(be_ref, nu_ref, tok_ref, h_hbm, wg_ref, wu_ref, wd_ref, ys_ref,
                   rows_ref, x_ref, acc_ref, sem, *, tm):
    i = pl.program_id(0)
    j = pl.program_id(1)
    nj = pl.num_programs(1)
    used = i < nu_ref[0]

    @pl.when(used & (j == 0))
    def _gather():
        base = i * tm

        def row(r):
            return pltpu.make_async_copy(h_hbm.at[pl.ds(tok_ref[base + r], 1)],
                                         rows_ref.at[pl.ds(r, 1)], sem)

        def start(r, _):
            row(r).start()
            return 0

        lax.fori_loop(0, tm, start, 0)

        def wait(r, _):
            row(r).wait()
            return 0

        lax.fori_loop(0, tm, wait, 0)
        x_ref[...] = rows_ref[...].astype(BF16)

    @pl.when(used)
    def _compute():
        x = x_ref[...]
        gt = jnp.dot(x, wg_ref[0], preferred_element_type=F32)
        up = jnp.dot(x, wu_ref[0], preferred_element_type=F32)
        act = (gt * jax.nn.sigmoid(gt) * up).astype(BF16)
        t = jnp.dot(act, wd_ref[0], preferred_element_type=F32)

        @pl.when(j == 0)
        def _():
            acc_ref[...] = t

        @pl.when(j > 0)
        def _():
            acc_ref[...] += t

        @pl.when(j == nj - 1)
        def _():
            ys_ref[...] = acc_ref[...]

    @pl.when(jnp.logical_not(used) & (j == nj - 1))
    def _unused():
        ys_ref[...] = jnp.zeros_like(ys_ref)


def _experts(block_e, n_used, slot_tok, h2d, wg, wu, wd):
    m_pad = slot_tok.shape[0]
    tm = TM_EXP
    tf = TF_EXP
    nj = D_FF_EXPERT // tf

    def jj(i, j, nu):
        return jnp.where(i < nu[0], j, nj - 1)

    return pl.pallas_call(
        functools.partial(_expert_kernel, tm=tm),
        grid_spec=pltpu.PrefetchScalarGridSpec(
            num_scalar_prefetch=3,
            grid=(m_pad // tm, nj),
            in_specs=[
                pl.BlockSpec(memory_space=pl.ANY),
                pl.BlockSpec((1, D_MODEL, tf), lambda i, j, be, nu, tok: (be[i], 0, jj(i, j, nu))),
                pl.BlockSpec((1, D_MODEL, tf), lambda i, j, be, nu, tok: (be[i], 0, jj(i, j, nu))),
                pl.BlockSpec((1, tf, D_MODEL), lambda i, j, be, nu, tok: (be[i], jj(i, j, nu), 0)),
            ],
            out_specs=pl.BlockSpec((tm, D_MODEL), lambda i, j, be, nu, tok: (i, 0)),
            scratch_shapes=[pltpu.VMEM((tm, D_MODEL), F32), pltpu.VMEM((tm, D_MODEL), BF16),
                            pltpu.VMEM((tm, D_MODEL), F32), pltpu.SemaphoreType.DMA(())],
        ),
        out_shape=jax.ShapeDtypeStruct((m_pad, D_MODEL), F32),
        compiler_params=_cparams("arbitrary", "arbitrary"),
        name="experts",
    )(block_e, n_used, slot_tok, h2d, wg, wu, wd)


def _combine_kernel(dest_ref, x1_ref, gate_ref, ys_hbm, mod_ref, modn_ref, lnw_ref, lnb_ref,
                    x2_ref, u_ref, buf_ref, sem, *, tm, s_len):
    base = pl.program_id(0) * s_len + pl.program_id(1) * tm

    def row(r, k):
        t = base + r
        return pltpu.make_async_copy(ys_hbm.at[pl.ds(dest_ref[2 * t + k], 1)],
                                     buf_ref.at[k, pl.ds(r, 1)], sem)

    def start(r, _):
        row(r, 0).start()
        row(r, 1).start()
        return 0

    lax.fori_loop(0, tm, start, 0)

    def wait(r, _):
        row(r, 0).wait()
        row(r, 1).wait()
        return 0

    lax.fori_loop(0, tm, wait, 0)
    gate = gate_ref[0]
    f = gate[:, 0:1] * buf_ref[0] + gate[:, 1:2] * buf_ref[1]
    _write_epilogue(x1_ref[0], f, mod_ref[0], modn_ref, lnw_ref, lnb_ref, x2_ref, u_ref)


def _combine(dest_flat, x1, gates, ys, mod, modn, ln_w, ln_b):
    B, S, _ = x1.shape
    tm = min(TM_COMB, S)
    rows = pl.BlockSpec((1, tm, D_MODEL), lambda b, i, d: (b, i, 0))
    modspec = pl.BlockSpec((1, 6, D_MODEL), lambda b, i, d: (b, 0, 0))
    cst = pl.BlockSpec((1, D_MODEL), lambda b, i, d: (0, 0))
    return pl.pallas_call(
        functools.partial(_combine_kernel, tm=tm, s_len=S),
        grid_spec=pltpu.PrefetchScalarGridSpec(
            num_scalar_prefetch=1,
            grid=(B, S // tm),
            in_specs=[rows, pl.BlockSpec((1, tm, TOP_K), lambda b, i, d: (b, i, 0)),
                      pl.BlockSpec(memory_space=pl.ANY), modspec, modspec, cst, cst],
            out_specs=[rows, rows],
            scratch_shapes=[pltpu.VMEM((TOP_K, tm, D_MODEL), F32), pltpu.SemaphoreType.DMA(())],
        ),
        out_shape=[
            jax.ShapeDtypeStruct((B, S, D_MODEL), F32),
            jax.ShapeDtypeStruct((B, S, D_MODEL), BF16),
        ],
        compiler_params=_cparams("arbitrary", "arbitrary"),
        name="combine",
    )(dest_flat, x1, gates, ys, mod, modn, ln_w, ln_b)


def _moe(x1, h, rw_pad, rb_pad, ltri, wg, wu, wd, mod, modn, ln_w, ln_b):
    B, S, _ = x1.shape
    n = B * S
    m = n * TOP_K
    rec, cnt = _router(h, rw_pad, rb_pad, ltri)
    rec = rec.reshape(n, LANES)
    e = rec[:, 0:TOP_K].astype(jnp.int32)
    gates = rec[:, TOP_K:2 * TOP_K]
    rank = rec[:, 2 * TOP_K:3 * TOP_K].astype(jnp.int32)
    counts = cnt[0, :N_EXPERTS].astype(jnp.int32)
    padded = (counts + TM_EXP - 1) // TM_EXP * TM_EXP
    pad_end = jnp.cumsum(padded)
    pad_start = pad_end - padded
    dest = (pad_start[e] + rank).reshape(m)
    n_blocks = m // TM_EXP + N_EXPERTS
    block_e = jnp.minimum(
        jnp.searchsorted(pad_end, jnp.arange(n_blocks, dtype=jnp.int32) * TM_EXP, side="right"),
        N_EXPERTS - 1).astype(jnp.int32)
    n_used = (pad_end[-1:] // TM_EXP).astype(jnp.int32)
    slot_tok = jnp.zeros((n_blocks * TM_EXP,), jnp.int32).at[dest].set(
        jnp.arange(m, dtype=jnp.int32) // TOP_K)
    ys = _experts(block_e, n_used, slot_tok, h.reshape(n, D_MODEL), wg, wu, wd)
    return _combine(dest, x1, gates.reshape(B, S, TOP_K), ys, mod, modn, ln_w, ln_b)


def _mla_weights(w_uq, w_ukv):
    L = w_uq.shape[0]
    H = MLA_HEADS
    half = QK_ROPE // 2
    wq = w_uq.reshape(L, Q_LORA, H, QK_NOPE + QK_ROPE)
    nope, rope = wq[..., :QK_NOPE], wq[..., QK_NOPE:]
    zpad = jnp.zeros((L, Q_LORA, H, HEAD_PAD - QK_NOPE - QK_ROPE), w_uq.dtype)
    wq_pad = jnp.concatenate([nope, rope, zpad], -1).reshape(L, Q_LORA, H * HEAD_PAD)
    rope_rot = jnp.concatenate([-rope[..., half:], rope[..., :half]], -1)
    wq_rot = jnp.concatenate([jnp.zeros_like(nope), rope_rot, zpad], -1).reshape(L, Q_LORA, H * HEAD_PAD)
    wkv = w_ukv.reshape(L, KV_LORA, H, QK_NOPE + V_HEAD)
    k_nope, v = wkv[..., :QK_NOPE], wkv[..., QK_NOPE:]
    wk_pad = jnp.concatenate(
        [k_nope, jnp.zeros((L, KV_LORA, H, HEAD_PAD - QK_NOPE), w_ukv.dtype)], -1
    ).reshape(L, KV_LORA, H * HEAD_PAD)
    wv = v.reshape(L, KV_LORA, MLA_DIM)
    return wq_pad.astype(BF16), wq_rot.astype(BF16), wk_pad.astype(BF16), wv.astype(BF16)


def _rope_placement():
    half = QK_ROPE // 2
    src = jnp.arange(QK_ROPE)[:, None]
    dst = jnp.arange(HEAD_PAD)[None, :]
    place = (dst == src + QK_NOPE).astype(BF16)
    rot = jnp.where((src < half) & (dst == src + QK_NOPE + half), 1.0,
                    jnp.where((src >= half) & (dst == src + QK_NOPE - half), -1.0, 0.0))
    return place, rot.astype(BF16)


def kernel(x, c, positions, w_ada, b_ada, w_in, shift_mu, decay_w0, decay_up, aaa_a0, aaa_up, gate_up, k_k, k_a, r_k, gn_w, gn_b, q_norm_w, w_uq, kv_norm_w, w_ukv, p_a, p_b, w_o, ln1_w, ln1_b, ln2_w, ln2_b, ffn_w_gate, ffn_w_up, ffn_w_down, router_w, router_b, moe_w_gate, moe_w_up, moe_w_down):
    B, S, _ = x.shape
    n = B * S
    L = DEPTH
    assert S % RW_CHUNK == 0 and S % LANES == 0

    mod = _ada(c, w_ada, b_ada).reshape(L, B, 6, D_MODEL)
    posf = positions.astype(F32).reshape(B, S, 1)

    w_shift = w_in[:, :, :SHIFT_DIM].astype(BF16)
    w_lat = jnp.pad(w_in[:, :, SHIFT_DIM:SHIFT_DIM + LAT_DIM],
                    ((0, 0), (0, 0), (0, LAT_PAD - LAT_DIM))).astype(BF16)
    w_gates = w_in[:, :, SHIFT_DIM + LAT_DIM:].astype(BF16)
    wq_pad, wq_rot, wk_pad, wv = _mla_weights(w_uq, w_ukv)
    place, placer = _rope_placement()
    lane_head = jnp.arange(RW_DIM) // RW_HEAD
    hsum = (lane_head[:, None] == lane_head[None, :]).astype(BF16)
    row1 = lambda a, l: a[l].reshape(1, -1)
    tm_r = min(TM_ROUTE, S)
    ltri = (jnp.arange(tm_r)[None, :] < jnp.arange(tm_r)[:, None]).astype(BF16)
    rw_pad = jnp.pad(router_w, ((0, 0), (0, 0), (0, LANES - N_EXPERTS)))
    rb_pad = jnp.pad(router_b, ((0, 0), (0, LANES - N_EXPERTS)), constant_values=NEG_BIG)

    u = _lnmod(x, mod[0])
    for l in range(L):
        u2 = u.reshape(n, D_MODEL)
        ps = _mm(u2, w_shift[l], F32).reshape(B, S, SHIFT_DIM)
        lat = _mm(u2, w_lat[l], F32).reshape(B, S, LAT_PAD)
        sg = _mm(u2, w_gates[l], BF16, act="sigmoid").reshape(B, S, 2 * D_MODEL)
        ya = _rwkv(ps, row1(shift_mu, l), row1(decay_w0, l), decay_up[l].astype(BF16), row1(aaa_a0, l),
                   aaa_up[l].astype(BF16), gate_up[l].astype(BF16), row1(k_k, l), row1(k_a, l),
                   row1(r_k, l), row1(gn_w, l), row1(gn_b, l), hsum)
        q, k, v = _mla_prep(lat, posf, row1(q_norm_w, l), wq_pad[l], wq_rot[l], row1(kv_norm_w, l),
                            wk_pad[l], wv[l], place, placer)
        yb = _attn(q, k, v)
        moe_layer = l % 2 == 1
        x1, h = _merge_out(x, ya, yb, sg, p_a[l].astype(BF16), p_b[l].astype(BF16), w_o[l].astype(BF16),
                           mod[l], row1(ln1_w, l), row1(ln1_b, l), F32 if moe_layer else BF16)
        modn = mod[min(l + 1, L - 1)]
        if moe_layer:
            x, u = _moe(x1, h, rw_pad[l // 2], rb_pad[l // 2].reshape(1, LANES), ltri,
                        moe_w_gate[l // 2].astype(BF16), moe_w_up[l // 2].astype(BF16),
                        moe_w_down[l // 2].astype(BF16), mod[l], modn, row1(ln2_w, l), row1(ln2_b, l))
        else:
            x, u = _ffn(x1, h, ffn_w_gate[l // 2].astype(BF16), ffn_w_up[l // 2].astype(BF16),
                        ffn_w_down[l // 2].astype(BF16), mod[l], modn, row1(ln2_w, l), row1(ln2_b, l))
    return x
```

```python
import functools
import math

import jax
import jax.numpy as jnp
from jax import lax
from jax.experimental import pallas as pl
from jax.experimental.pallas import tpu as pltpu

F32 = jnp.float32
BF16 = jnp.bfloat16

D_MODEL = 1024
DEPTH = 4
RW_HEADS = 8
RW_HEAD = 64
RW_DIM = RW_HEADS * RW_HEAD
DECAY_LORA = 64
AAA_LORA = 64
GATE_LORA = 128
GN_EPS = 64e-5
MLA_HEADS = 8
QK_NOPE = 64
QK_ROPE = 32
V_HEAD = 64
Q_LORA = 384
KV_LORA = 256
MLA_DIM = MLA_HEADS * V_HEAD
ROPE_THETA = 10000.0
SHIFT_DIM = 3 * RW_DIM + DECAY_LORA + AAA_LORA + GATE_LORA
LAT_DIM = Q_LORA + KV_LORA + QK_ROPE
LAT_PAD = 768
D_FF = 2816
N_EXPERTS = 8
TOP_K = 2
D_FF_EXPERT = 3584
ALPHA = (2.0 * DEPTH) ** 0.25
LN_EPS = 1e-5
RMS_EPS = 1e-6
LANES = 128
HEAD_PAD = 128
NEG_BIG = -1e30

RW_CHUNK = 64
RW_SEQS = 4
TM_MM = 512
TM_PREP = 256
TQ = 512
TM_MERGE = 512
TM_FFN = 512
FF_CHUNK = 256
TM_ROUTE = 512
TM_EXP = 512
EXP_NJ = 2
TM_COMB = 256

_HI = lax.Precision.HIGHEST


def _cparams(*sem):
    return pltpu.CompilerParams(dimension_semantics=sem)


def _dot(a, b):
    return jnp.dot(a.astype(BF16), b.astype(BF16), preferred_element_type=F32)


def _dot_nt(a, b):
    return lax.dot_general(a.astype(BF16), b.astype(BF16), (((1,), (1,)), ((), ())),
                           preferred_element_type=F32)


def _dot_tn(a, b):
    return lax.dot_general(a.astype(BF16), b.astype(BF16), (((0,), (0,)), ((), ())),
                           preferred_element_type=F32)


def _split(x, n):
    parts = []
    for _ in range(n - 1):
        p = x.astype(BF16)
        parts.append(p)
        x = x - p.astype(F32)
    parts.append(x.astype(BF16))
    return parts


def _dot_split_lhs(x, w_bf16, n):
    acc = None
    for p in _split(x, n):
        t = jnp.dot(p, w_bf16, preferred_element_type=F32)
        acc = t if acc is None else acc + t
    return acc


def _dot_split_rhs(w_bf16, x, n):
    acc = None
    for p in _split(x, n):
        t = jnp.dot(w_bf16, p, preferred_element_type=F32)
        acc = t if acc is None else acc + t
    return acc


def _layer_norm(x):
    mu = jnp.mean(x, -1, keepdims=True)
    xc = x - mu
    var = jnp.mean(xc * xc, -1, keepdims=True)
    return xc * lax.rsqrt(var + LN_EPS)


def _full(shape):
    nd = len(shape)
    return pl.BlockSpec(shape, lambda *_: (0,) * nd)


def _ada_kernel(c_ref, w_ref, b_ref, o_ref):
    c = c_ref[...]
    ca = c * jax.nn.sigmoid(c)
    o_ref[0] = jnp.dot(ca, w_ref[0], preferred_element_type=F32, precision=_HI) + b_ref[0]


def _ada(c, w_ada, b_ada):
    L, _, six_d = w_ada.shape
    B = c.shape[0]
    nj = six_d // D_MODEL
    return pl.pallas_call(
        _ada_kernel,
        grid=(L, nj),
        in_specs=[
            pl.BlockSpec((B, D_MODEL), lambda l, j: (0, 0)),
            pl.BlockSpec((1, D_MODEL, D_MODEL), lambda l, j: (l, 0, j)),
            pl.BlockSpec((1, 1, D_MODEL), lambda l, j: (l, 0, j)),
        ],
        out_specs=pl.BlockSpec((1, B, D_MODEL), lambda l, j: (l, 0, j)),
        out_shape=jax.ShapeDtypeStruct((L, B, six_d), F32),
        compiler_params=_cparams("parallel", "parallel"),
        name="ada",
    )(c, w_ada, b_ada.reshape(L, 1, six_d))


def _lnmod_kernel(x_ref, mod_ref, o_ref):
    m = mod_ref[0]
    o_ref[0] = (_layer_norm(x_ref[0]) * (1.0 + m[1:2]) + m[0:1]).astype(o_ref.dtype)


def _lnmod(x, mod):
    B, S, _ = x.shape
    tm = min(TM_MM, S)
    return pl.pallas_call(
        _lnmod_kernel,
        grid=(B, S // tm),
        in_specs=[
            pl.BlockSpec((1, tm, D_MODEL), lambda b, i: (b, i, 0)),
            pl.BlockSpec((1, 6, D_MODEL), lambda b, i: (b, 0, 0)),
        ],
        out_specs=pl.BlockSpec((1, tm, D_MODEL), lambda b, i: (b, i, 0)),
        out_shape=jax.ShapeDtypeStruct((B, S, D_MODEL), BF16),
        compiler_params=_cparams("parallel", "parallel"),
        name="lnmod",
    )(x, mod)


def _mm_kernel(a_ref, w_ref, o_ref, *, act):
    acc = jnp.dot(a_ref[...], w_ref[...], preferred_element_type=F32)
    if act == "sigmoid":
        acc = jax.nn.sigmoid(acc)
    o_ref[...] = acc.astype(o_ref.dtype)


def _mm(a, w, out_dtype, act=None):
    n, k = a.shape
    n_out = w.shape[1]
    tm = min(TM_MM, n)
    return pl.pallas_call(
        functools.partial(_mm_kernel, act=act),
        grid=(n // tm,),
        in_specs=[
            pl.BlockSpec((tm, k), lambda i: (i, 0)),
            pl.BlockSpec((k, n_out), lambda i: (0, 0)),
        ],
        out_specs=pl.BlockSpec((tm, n_out), lambda i: (i, 0)),
        out_shape=jax.ShapeDtypeStruct((n, n_out), out_dtype),
        compiler_params=_cparams("parallel"),
        name="mm",
    )(a, w)


def _rwkv_kernel(ps_ref, mu_ref, w0_ref, dup_ref, a0_ref, aup_ref, gup_ref, kk_ref, ka_ref, rk_ref,
                 gnw_ref, gnb_ref, hsum_ref, o_ref, state_ref, prev_ref, y_ref):
    C = RW_CHUNK
    N = RW_HEAD
    G = ps_ref.shape[0]

    @pl.when(pl.program_id(1) == 0)
    def _init():
        state_ref[...] = jnp.zeros_like(state_ref)
        prev_ref[...] = jnp.zeros_like(prev_ref)

    half = RW_DIM // 2
    hs_lo = hsum_ref[0:half, 0:half]
    hs_hi = hsum_ref[half:, half:]

    def head_sum(z):
        zb = z.astype(BF16)
        return jnp.concatenate([jnp.dot(zb[:, 0:half], hs_lo, preferred_element_type=F32),
                                jnp.dot(zb[:, half:], hs_hi, preferred_element_type=F32)], axis=1)

    ri = lax.broadcasted_iota(jnp.int32, (C, C), 0)
    ci = lax.broadcasted_iota(jnp.int32, (C, C), 1)
    tri = (ci <= ri).astype(BF16)
    r4 = lax.broadcasted_iota(jnp.int32, (2 * C, 4 * C), 0)
    c4 = lax.broadcasted_iota(jnp.int32, (2 * C, 4 * C), 1)
    keep = (c4 % C) < (r4 % C) + (r4 >= C).astype(jnp.int32)
    r1 = lax.broadcasted_iota(jnp.int32, (C, 2 * C), 0)
    c1 = lax.broadcasted_iota(jnp.int32, (C, 2 * C), 1)
    low = c1 < C
    eye_hi = (c1 == r1 + C).astype(F32)

    prep = []
    for s in range(G):
        p = ps_ref[s]
        row = lax.broadcasted_iota(jnp.int32, p.shape, 0)
        shifted = jnp.where(row == 0, prev_ref[s, 0:1, :], pltpu.roll(p, 1, 0))
        prev_ref[s, 0:1, :] = p[C - 1:C, :]
        xs = p + (shifted - p) * mu_ref[...]
        r = xs[:, 0:RW_DIM]
        k = xs[:, RW_DIM:2 * RW_DIM]
        v = xs[:, 2 * RW_DIM:3 * RW_DIM]
        o = 3 * RW_DIM
        dw = xs[:, o:o + DECAY_LORA]
        da = xs[:, o + DECAY_LORA:o + DECAY_LORA + AAA_LORA]
        dg = xs[:, o + DECAY_LORA + AAA_LORA:SHIFT_DIM]
        lw = -math.exp(-0.5) * jax.nn.sigmoid(w0_ref[...] + _dot(jnp.tanh(dw), dup_ref[...]))
        a = jax.nn.sigmoid(a0_ref[...] + _dot(da, aup_ref[...]))
        gate = _dot(jax.nn.sigmoid(dg), gup_ref[...])
        kk = k * kk_ref[...]
        kk = kk * lax.rsqrt(jnp.maximum(head_sum(kk * kk), 1e-24))
        k_mod = k * (1.0 + (a - 1.0) * ka_ref[...])
        b = kk * a
        bonus = head_sum(r * k_mod * rk_ref[...]) * v
        cw = _dot_split_rhs(tri, lw, 2)
        cw_last = cw[C - 1:C, :]
        e_neg = jnp.exp(-cw)
        e_rem = jnp.exp(cw_last - cw)
        prep.append(dict(
            kp=(kk * jnp.exp(cw - lw)).astype(BF16), rp=(r * jnp.exp(cw)).astype(BF16),
            bm=(b * e_neg).astype(BF16), km=(k_mod * e_neg).astype(BF16),
            bt=(b * e_rem).astype(BF16), kt=(k_mod * e_rem).astype(BF16),
            v=v.astype(BF16), w_c=jnp.exp(cw_last), bonus=bonus, gate=gate))

    chains = [(s, h) for s in range(G) for h in range(RW_HEADS)]
    sl = lambda h: slice(h * N, (h + 1) * N)

    krs, aas = [], []
    for s, h in chains:
        d = prep[s]
        kr = jnp.concatenate([d["kp"][:, sl(h)], d["rp"][:, sl(h)]], axis=0)
        bk = jnp.concatenate([d["bm"][:, sl(h)], d["km"][:, sl(h)], d["km"][:, sl(h)], d["bm"][:, sl(h)]],
                             axis=0)
        krs.append(kr)
        aas.append(jnp.where(keep, _dot_nt(kr, bk), 0.0))
    s0s = [state_ref[s, h] for s, h in chains]
    xxs = [_dot_nt(kr, s0) for kr, s0 in zip(krs, s0s)]
    rts = [jnp.where(low, _dot(aa[0:C, 0:C], aa[0:C, 0:2 * C]), eye_hi - aa[0:C, 2 * C:4 * C]) for aa in aas]
    for _ in range(int(math.log2(C)) - 1):
        outs = [_dot(rt[:, 0:C], rt) for rt in rts]
        rts = [jnp.where(low, out, rt + out) for rt, out in zip(rts, outs)]
    vs = [prep[s]["v"][:, sl(h)] for s, h in chains]
    gms = [xx[0:C] + _dot(aa[0:C, 2 * C:3 * C], v_h) for xx, aa, v_h in zip(xxs, aas, vs)]
    us = [-_dot(rt[:, C:2 * C], gm) for rt, gm in zip(rts, gms)]
    uvs = [jnp.concatenate([u.astype(BF16), v_h], axis=0) for u, v_h in zip(us, vs)]
    ys = [xx[C:2 * C] + _dot(aa[C:2 * C, 0:2 * C], uv) for xx, aa, uv in zip(xxs, aas, uvs)]
    news = []
    for (s, h), s0, uv in zip(chains, s0s, uvs):
        d = prep[s]
        btkt = jnp.concatenate([d["bt"][:, sl(h)], d["kt"][:, sl(h)]], axis=0)
        news.append(s0 * d["w_c"][:, sl(h)] + _dot_tn(uv, btkt))
    for (s, h), y_h, new in zip(chains, ys, news):
        y_ref[s, :, sl(h)] = y_h
        state_ref[s, h] = new

    for s in range(G):
        d = prep[s]
        y = y_ref[s]
        mean = head_sum(y) * (1.0 / N)
        yc = y - mean
        var = head_sum(yc * yc) * (1.0 / N)
        yn = yc * lax.rsqrt(var + GN_EPS) * gnw_ref[...] + gnb_ref[...]
        o_ref[s] = ((yn + d["bonus"]) * d["gate"]).astype(o_ref.dtype)


def _rwkv(ps, mu, w0, dup, a0, aup, gup, k_k, k_a, r_k, gn_w, gn_b, hsum):
    B, S, _ = ps.shape
    C = RW_CHUNK
    G = RW_SEQS if B % RW_SEQS == 0 else 1
    row = lambda n: pl.BlockSpec((1, n), lambda b, j: (0, 0))
    return pl.pallas_call(
        _rwkv_kernel,
        grid=(B // G, S // C),
        in_specs=[
            pl.BlockSpec((G, C, SHIFT_DIM), lambda b, j: (b, j, 0)),
            row(SHIFT_DIM), row(RW_DIM),
            pl.BlockSpec((DECAY_LORA, RW_DIM), lambda b, j: (0, 0)),
            row(RW_DIM),
            pl.BlockSpec((AAA_LORA, RW_DIM), lambda b, j: (0, 0)),
            pl.BlockSpec((GATE_LORA, RW_DIM), lambda b, j: (0, 0)),
            row(RW_DIM), row(RW_DIM), row(RW_DIM), row(RW_DIM), row(RW_DIM),
            pl.BlockSpec((RW_DIM, RW_DIM), lambda b, j: (0, 0)),
        ],
        out_specs=pl.BlockSpec((G, C, RW_DIM), lambda b, j: (b, j, 0)),
        out_shape=jax.ShapeDtypeStruct((B, S, RW_DIM), BF16),
        scratch_shapes=[
            pltpu.VMEM((G, RW_HEADS, RW_HEAD, RW_HEAD), F32),
            pltpu.VMEM((G, 8, SHIFT_DIM), F32),
            pltpu.VMEM((G, C, RW_DIM), F32),
        ],
        compiler_params=_cparams("parallel", "arbitrary"),
        name="rwkv",
    )(ps, mu, w0, dup, a0, aup, gup, k_k, k_a, r_k, gn_w, gn_b, hsum)


def _mla_prep_kernel(lat_ref, pos_ref, qnw_ref, wq_ref, wqr_ref, kvnw_ref, wk_ref, wv_ref,
                     place_ref, placer_ref, q_ref, k_ref, v_ref):
    lat = lat_ref[0]
    c_q = lat[:, 0:Q_LORA]
    c_kv = lat[:, Q_LORA:Q_LORA + KV_LORA]
    k_rope = lat[:, Q_LORA + KV_LORA:LAT_DIM]

    def rms(z, w):
        return z * lax.rsqrt(jnp.mean(z * z, -1, keepdims=True) + RMS_EPS) * w

    half = QK_ROPE // 2
    lane = lax.broadcasted_iota(jnp.int32, (1, HEAD_PAD), 1)
    in_rope = (lane >= QK_NOPE) & (lane < QK_NOPE + QK_ROPE)
    fidx = jnp.where(lane < QK_NOPE + half, lane - QK_NOPE, lane - QK_NOPE - half)
    inv_freq = jnp.exp(-math.log(ROPE_THETA) * fidx.astype(F32) / half)
    ang = pos_ref[0] * inv_freq
    cos_r = jnp.where(in_rope, jnp.cos(ang), 0.0)
    sin_r = jnp.where(in_rope, jnp.sin(ang), 0.0)
    scale = (QK_NOPE + QK_ROPE) ** -0.5
    q_mul = jnp.where(lane < QK_NOPE, 1.0, cos_r) * scale
    q_mul_rot = sin_r * scale

    cq_n = rms(c_q, qnw_ref[...]).astype(BF16)
    q_all = jnp.dot(cq_n, wq_ref[...], preferred_element_type=F32)
    q_rot = jnp.dot(cq_n, wqr_ref[...], preferred_element_type=F32)
    ckv_n = rms(c_kv, kvnw_ref[...]).astype(BF16)
    k_all = jnp.dot(ckv_n, wk_ref[...], preferred_element_type=F32)
    v_all_t = lax.dot_general(wv_ref[...], ckv_n, (((1,), (1,)), ((), ())), preferred_element_type=F32)
    kr_pad = (_dot_split_lhs(k_rope, place_ref[...], 3) * cos_r
              + _dot_split_lhs(k_rope, placer_ref[...], 3) * sin_r)
    for h in range(MLA_HEADS):
        sl = slice(h * HEAD_PAD, (h + 1) * HEAD_PAD)
        q_ref[0, h] = (q_all[:, sl] * q_mul + q_rot[:, sl] * q_mul_rot).astype(q_ref.dtype)
        k_ref[0, h] = (k_all[:, sl] + kr_pad).astype(k_ref.dtype)
    for hp in range(MLA_HEADS // 2):
        v_ref[0, hp] = v_all_t[hp * LANES:(hp + 1) * LANES, :].astype(v_ref.dtype)


def _mla_prep(lat, posf, qnw, wq, wqr, kvnw, wk, wv, place, placer):
    B, S, _ = lat.shape
    tm = min(TM_PREP, S)
    H = MLA_HEADS
    cst = lambda shape: pl.BlockSpec(shape, lambda b, i: (0,) * len(shape))
    return pl.pallas_call(
        _mla_prep_kernel,
        grid=(B, S // tm),
        in_specs=[
            pl.BlockSpec((1, tm, LAT_PAD), lambda b, i: (b, i, 0)),
            pl.BlockSpec((1, tm, 1), lambda b, i: (b, i, 0)),
            cst((1, Q_LORA)), cst((Q_LORA, H * HEAD_PAD)), cst((Q_LORA, H * HEAD_PAD)),
            cst((1, KV_LORA)), cst((KV_LORA, H * HEAD_PAD)), cst((MLA_DIM, KV_LORA)),
            cst((QK_ROPE, HEAD_PAD)), cst((QK_ROPE, HEAD_PAD)),
        ],
        out_specs=[
            pl.BlockSpec((1, H, tm, HEAD_PAD), lambda b, i: (b, 0, i, 0)),
            pl.BlockSpec((1, H, tm, HEAD_PAD), lambda b, i: (b, 0, i, 0)),
            pl.BlockSpec((1, H // 2, LANES, tm), lambda b, i: (b, 0, 0, i)),
        ],
        out_shape=[
            jax.ShapeDtypeStruct((B, H, S, HEAD_PAD), BF16),
            jax.ShapeDtypeStruct((B, H, S, HEAD_PAD), BF16),
            jax.ShapeDtypeStruct((B, H // 2, LANES, S), BF16),
        ],
        compiler_params=_cparams("parallel", "parallel"),
        name="mla_prep",
    )(lat, posf, qnw, wq, wqr, kvnw, wk, wv, place, placer)


def _attn_kernel(q_ref, k_ref, vt_ref, o_ref, *, tq):
    i = pl.program_id(2)
    ki = lax.broadcasted_iota(jnp.int32, (tq, tq), 0)
    qi = lax.broadcasted_iota(jnp.int32, (tq, tq), 1)
    row_v = lax.broadcasted_iota(jnp.int32, (LANES, tq), 0)
    one = jnp.ones((), BF16)
    qs = [q_ref[0, hh] for hh in range(2)]

    def step(j, carry, masked):
        start = pl.multiple_of(j * tq, tq)
        vt = vt_ref[0, 0, :, pl.ds(start, tq)]
        v_aug = [jnp.where(row_v < V_HEAD, vt, one), jnp.where(row_v < V_HEAD, one, vt)]
        ss = [lax.dot_general(k_ref[0, hh, pl.ds(start, tq), :], qs[hh], (((1,), (1,)), ((), ())),
                              preferred_element_type=F32) for hh in range(2)]
        new = []
        ps = []
        for hh in range(2):
            m, acc = carry[hh]
            s = ss[hh]
            if masked:
                s = jnp.where(ki <= qi, s, NEG_BIG)
            m_new = jnp.maximum(m, jnp.max(s, 0, keepdims=True))
            ps.append(jnp.exp(s - m_new).astype(BF16))
            new.append((m_new, jnp.exp(m - m_new) * acc))
        pvs = [jnp.dot(v_aug[hh], ps[hh], preferred_element_type=F32) for hh in range(2)]
        return tuple((new[hh][0], new[hh][1] + pvs[hh]) for hh in range(2))

    init1 = (jnp.full((1, tq), NEG_BIG, F32), jnp.zeros((LANES, tq), F32))
    carry = lax.fori_loop(0, i, functools.partial(step, masked=False), (init1, init1))
    carry = step(i, carry, True)
    a0 = carry[0][1]
    a1 = carry[1][1]
    out_t = jnp.concatenate([a0[0:V_HEAD] / a0[V_HEAD:], a1[V_HEAD:] / a1[0:V_HEAD]], axis=0)
    o_ref[0] = out_t.T.astype(o_ref.dtype)


def _attn(q, k, vt):
    B, H, S, _ = q.shape
    tq = min(TQ, S)
    return pl.pallas_call(
        functools.partial(_attn_kernel, tq=tq),
        grid=(B, H // 2, S // tq),
        in_specs=[
            pl.BlockSpec((1, 2, tq, HEAD_PAD), lambda b, p, i: (b, p, i, 0)),
            pl.BlockSpec((1, 2, S, HEAD_PAD), lambda b, p, i: (b, p, 0, 0)),
            pl.BlockSpec((1, 1, LANES, S), lambda b, p, i: (b, p, 0, 0)),
        ],
        out_specs=pl.BlockSpec((1, tq, LANES), lambda b, p, i: (b, i, p)),
        out_shape=jax.ShapeDtypeStruct((B, S, MLA_DIM), BF16),
        compiler_params=_cparams("parallel", "parallel", "arbitrary"),
        name="attn",
    )(q, k, vt)


def _residual_epilogue(x, f, gate, ln_w, ln_b):
    return _layer_norm(ALPHA * x + gate * f) * ln_w + ln_b


def _merge_out_kernel(x_ref, ya_ref, yb_ref, sga_ref, sgb_ref, pa_ref, pb_ref, wo_ref, mod_ref,
                      lnw_ref, lnb_ref, x1_ref, h_ref):
    m = mod_ref[0]
    ma = jnp.dot(ya_ref[0], pa_ref[...], preferred_element_type=F32)
    mb = jnp.dot(yb_ref[0], pb_ref[...], preferred_element_type=F32)
    merged = sga_ref[0].astype(F32) * ma + sgb_ref[0].astype(F32) * mb
    mix = jnp.dot(merged.astype(BF16), wo_ref[...], preferred_element_type=F32)
    x1 = _residual_epilogue(x_ref[0], mix, m[2:3], lnw_ref[...], lnb_ref[...])
    x1_ref[0] = x1
    h_ref[0] = (_layer_norm(x1) * (1.0 + m[4:5]) + m[3:4]).astype(h_ref.dtype)


def _merge_out(x, ya, yb, sg, pa, pb, wo, mod, ln_w, ln_b, h_dtype):
    B, S, _ = x.shape
    tm = min(TM_MERGE, S)
    cst = lambda shape: pl.BlockSpec(shape, lambda b, i: (0,) * len(shape))
    rows = lambda n: pl.BlockSpec((1, tm, n), lambda b, i: (b, i, 0))
    return pl.pallas_call(
        _merge_out_kernel,
        grid=(B, S // tm),
        in_specs=[
            rows(D_MODEL), rows(RW_DIM), rows(MLA_DIM),
            pl.BlockSpec((1, tm, D_MODEL), lambda b, i: (b, i, 0)),
            pl.BlockSpec((1, tm, D_MODEL), lambda b, i: (b, i, 1)),
            cst((RW_DIM, D_MODEL)), cst((MLA_DIM, D_MODEL)), cst((D_MODEL, D_MODEL)),
            pl.BlockSpec((1, 6, D_MODEL), lambda b, i: (b, 0, 0)),
            cst((1, D_MODEL)), cst((1, D_MODEL)),
        ],
        out_specs=[rows(D_MODEL), rows(D_MODEL)],
        out_shape=[
            jax.ShapeDtypeStruct((B, S, D_MODEL), F32),
            jax.ShapeDtypeStruct((B, S, D_MODEL), h_dtype),
        ],
        compiler_params=_cparams("parallel", "parallel"),
        name="merge_out",
    )(x, ya, yb, sg, sg, pa, pb, wo, mod, ln_w, ln_b)


def _write_epilogue(x1, f, m, modn_ref, lnw_ref, lnb_ref, x2_ref, u_ref):
    x2 = _residual_epilogue(x1, f, m[5:6], lnw_ref[...], lnb_ref[...])
    x2_ref[0] = x2
    mn = modn_ref[0]
    u_ref[0] = (_layer_norm(x2) * (1.0 + mn[1:2]) + mn[0:1]).astype(u_ref.dtype)


def _ffn_kernel(x1_ref, h_ref, wg_ref, wu_ref, wd_ref, mod_ref, modn_ref, lnw_ref, lnb_ref,
                x2_ref, u_ref):
    h = h_ref[0]
    acc = None
    for j in range(D_FF // FF_CHUNK):
        sl = slice(j * FF_CHUNK, (j + 1) * FF_CHUNK)
        gt = jnp.dot(h, wg_ref[:, sl], preferred_element_type=F32)
        up = jnp.dot(h, wu_ref[:, sl], preferred_element_type=F32)
        act = (gt * jax.nn.sigmoid(gt) * up).astype(BF16)
        t = jnp.dot(act, wd_ref[sl, :], preferred_element_type=F32)
        acc = t if acc is None else acc + t
    _write_epilogue(x1_ref[0], acc, mod_ref[0], modn_ref, lnw_ref, lnb_ref, x2_ref, u_ref)


def _ffn(x1, h, wg, wu, wd, mod, modn, ln_w, ln_b):
    B, S, _ = x1.shape
    tm = min(TM_FFN, S)
    res = lambda shape: pl.BlockSpec(shape, lambda b, i: (0,) * len(shape), pipeline_mode=pl.Buffered(1))
    cst = lambda shape: pl.BlockSpec(shape, lambda b, i: (0,) * len(shape))
    rows = pl.BlockSpec((1, tm, D_MODEL), lambda b, i: (b, i, 0))
    modspec = pl.BlockSpec((1, 6, D_MODEL), lambda b, i: (b, 0, 0))
    return pl.pallas_call(
        _ffn_kernel,
        grid=(B, S // tm),
        in_specs=[rows, rows, res((D_MODEL, D_FF)), res((D_MODEL, D_FF)), res((D_FF, D_MODEL)),
                  modspec, modspec, cst((1, D_MODEL)), cst((1, D_MODEL))],
        out_specs=[rows, rows],
        out_shape=[
            jax.ShapeDtypeStruct((B, S, D_MODEL), F32),
            jax.ShapeDtypeStruct((B, S, D_MODEL), BF16),
        ],
        compiler_params=_cparams("parallel", "parallel"),
        name="ffn",
    )(x1, h, wg, wu, wd, mod, modn, ln_w, ln_b)


def _router_kernel(h_ref, rw_ref, rb_ref, ltri_ref, rec_ref, cnt_ref, carry_ref):
    first = (pl.program_id(0) == 0) & (pl.program_id(1) == 0)

    @pl.when(first)
    def _init():
        carry_ref[...] = jnp.zeros_like(carry_ref)

    h = h_ref[0]
    tm = h.shape[0]
    logits = jnp.dot(h, rw_ref[...], preferred_element_type=F32, precision=_HI) + rb_ref[...]
    lane = lax.broadcasted_iota(jnp.int32, (tm, LANES), 1)
    m1 = jnp.max(logits, -1, keepdims=True)
    i1 = jnp.min(jnp.where(logits == m1, lane, LANES), -1, keepdims=True)
    rest = jnp.where(lane == i1, -jnp.inf, logits)
    m2 = jnp.max(rest, -1, keepdims=True)
    i2 = jnp.min(jnp.where(rest == m2, lane, LANES), -1, keepdims=True)
    e21 = jnp.exp(m2 - m1)
    g1 = 1.0 / (1.0 + e21)
    g2 = e21 / (1.0 + e21)
    oh1 = lane == i1
    oh2 = lane == i2
    cnt = oh1.astype(F32) + oh2.astype(F32)
    before = jnp.dot(ltri_ref[...], cnt.astype(BF16), preferred_element_type=F32) + carry_ref[0:1, :]
    rank1 = jnp.sum(jnp.where(oh1, before, 0.0), -1, keepdims=True)
    rank2 = jnp.sum(jnp.where(oh2, before, 0.0), -1, keepdims=True)
    carry_ref[0:1, :] = carry_ref[0:1, :] + jnp.sum(cnt, 0, keepdims=True)
    cols = (i1.astype(F32), i2.astype(F32), g1, g2, rank1, rank2)
    rec = jnp.zeros((tm, LANES), F32)
    for c, val in enumerate(cols):
        rec = jnp.where(lane == c, val, rec)
    rec_ref[0] = rec
    cnt_ref[...] = carry_ref[...]


def _router(h, rw_pad, rb_pad, ltri):
    B, S, _ = h.shape
    tm = min(TM_ROUTE, S)
    return pl.pallas_call(
        _router_kernel,
        grid=(B, S // tm),
        in_specs=[
            pl.BlockSpec((1, tm, D_MODEL), lambda b, i: (b, i, 0)),
            pl.BlockSpec((D_MODEL, LANES), lambda b, i: (0, 0)),
            pl.BlockSpec((1, LANES), lambda b, i: (0, 0)),
            pl.BlockSpec((tm, tm), lambda b, i: (0, 0)),
        ],
        out_specs=[
            pl.BlockSpec((1, tm, LANES), lambda b, i: (b, i, 0)),
            pl.BlockSpec((8, LANES), lambda b, i: (0, 0)),
        ],
        out_shape=[
            jax.ShapeDtypeStruct((B, S, LANES), F32),
            jax.ShapeDtypeStruct((8, LANES), F32),
        ],
        scratch_shapes=[pltpu.VMEM((8, LANES), F32)],
        compiler_params=_cparams("arbitrary", "arbitrary"),
        name="router",
    )(h, rw_pad, rb_pad, ltri)


def _expert_kernel(be_ref, nu_ref, tok_ref, h_hbm, wg_ref, wu_ref, wd_ref, ys_ref,
                   rows_ref, x_ref, sems, *, tm):
    i = pl.program_id(0)
    j = pl.program_id(1)
    nb = pl.num_programs(0)
    nj = pl.num_programs(1)
    per_step = tm // EXP_NJ
    used = i < nu_ref[0]

    def row(block, slot, r):
        return pltpu.make_async_copy(h_hbm.at[pl.ds(tok_ref[block * tm + r], 1)],
                                     rows_ref.at[slot, pl.ds(r, 1)], sems.at[slot])

    def wait_block(slot):
        def wait(r, _):
            row(0, slot, r).wait()
            return 0
        lax.fori_loop(0, tm, wait, 0, unroll=8)

    @pl.when((i == 0) & (j == 0))
    def _first():
        def start(r, _):
            row(0, 0, r).start()
            return 0
        lax.fori_loop(0, tm, start, 0, unroll=8)

    @pl.when(j == 0)
    def _arrive():
        wait_block(i % 2)
        x_ref[...] = rows_ref[i % 2].astype(BF16)

    nxt = jnp.minimum(i + 1, nb - 1)
    for r in range(per_step):
        row(nxt, (i + 1) % 2, j * per_step + r).start()

    x = x_ref[...]
    acc = None
    for c in range(wg_ref.shape[2] // FF_CHUNK):
        cols = slice(c * FF_CHUNK, (c + 1) * FF_CHUNK)
        gt = jnp.dot(x, wg_ref[0, :, cols], preferred_element_type=F32)
        up = jnp.dot(x, wu_ref[0, :, cols], preferred_element_type=F32)
        act = (gt * jax.nn.sigmoid(gt) * up).astype(BF16)
        t = jnp.dot(act, wd_ref[0, cols, :], preferred_element_type=F32)
        acc = t if acc is None else acc + t
    acc = jnp.where(used, acc, 0.0)

    @pl.when(j == 0)
    def _():
        ys_ref[...] = acc

    @pl.when(j > 0)
    def _():
        ys_ref[...] += acc

    @pl.when((i == nb - 1) & (j == nj - 1))
    def _drain():
        wait_block((i + 1) % 2)


def _experts(block_e, n_used, slot_tok, h2d, wg, wu, wd):
    m_pad = slot_tok.shape[0]
    tm = TM_EXP
    nj = EXP_NJ
    tf = D_FF_EXPERT // nj

    def jj(i, j, nu):
        return jnp.where(i < nu[0], j, nj - 1)

    return pl.pallas_call(
        functools.partial(_expert_kernel, tm=tm),
        grid_spec=pltpu.PrefetchScalarGridSpec(
            num_scalar_prefetch=3,
            grid=(m_pad // tm, nj),
            in_specs=[
                pl.BlockSpec(memory_space=pl.ANY),
                pl.BlockSpec((1, D_MODEL, tf), lambda i, j, be, nu, tok: (be[i], 0, jj(i, j, nu))),
                pl.BlockSpec((1, D_MODEL, tf), lambda i, j, be, nu, tok: (be[i], 0, jj(i, j, nu))),
                pl.BlockSpec((1, tf, D_MODEL), lambda i, j, be, nu, tok: (be[i], jj(i, j, nu), 0)),
            ],
            out_specs=pl.BlockSpec((tm, D_MODEL), lambda i, j, be, nu, tok: (i, 0)),
            scratch_shapes=[pltpu.VMEM((2, tm, D_MODEL), F32), pltpu.VMEM((tm, D_MODEL), BF16),
                            pltpu.SemaphoreType.DMA((2,))],
        ),
        out_shape=jax.ShapeDtypeStruct((m_pad, D_MODEL), F32),
        compiler_params=_cparams("arbitrary", "arbitrary"),
        name="experts",
    )(block_e, n_used, slot_tok, h2d, wg, wu, wd)


def _combine_kernel(dest_ref, x1_ref, gate_ref, ys_hbm, mod_ref, modn_ref, lnw_ref, lnb_ref,
                    x2_ref, u_ref, buf_ref, sem, *, tm, s_len):
    base = pl.program_id(0) * s_len + pl.program_id(1) * tm

    def row(r, k):
        t = base + r
        return pltpu.make_async_copy(ys_hbm.at[pl.ds(dest_ref[2 * t + k], 1)],
                                     buf_ref.at[k, pl.ds(r, 1)], sem)

    def start(r, _):
        row(r, 0).start()
        row(r, 1).start()
        return 0

    lax.fori_loop(0, tm, start, 0, unroll=8)

    def wait(r, _):
        row(r, 0).wait()
        row(r, 1).wait()
        return 0

    lax.fori_loop(0, tm, wait, 0, unroll=8)
    gate = gate_ref[0]
    f = gate[:, 0:1] * buf_ref[0] + gate[:, 1:2] * buf_ref[1]
    _write_epilogue(x1_ref[0], f, mod_ref[0], modn_ref, lnw_ref, lnb_ref, x2_ref, u_ref)


def _combine(dest_flat, x1, gates, ys, mod, modn, ln_w, ln_b):
    B, S, _ = x1.shape
    tm = min(TM_COMB, S)
    rows = pl.BlockSpec((1, tm, D_MODEL), lambda b, i, d: (b, i, 0))
    modspec = pl.BlockSpec((1, 6, D_MODEL), lambda b, i, d: (b, 0, 0))
    cst = pl.BlockSpec((1, D_MODEL), lambda b, i, d: (0, 0))
    return pl.pallas_call(
        functools.partial(_combine_kernel, tm=tm, s_len=S),
        grid_spec=pltpu.PrefetchScalarGridSpec(
            num_scalar_prefetch=1,
            grid=(B, S // tm),
            in_specs=[rows, pl.BlockSpec((1, tm, TOP_K), lambda b, i, d: (b, i, 0)),
                      pl.BlockSpec(memory_space=pl.ANY), modspec, modspec, cst, cst],
            out_specs=[rows, rows],
            scratch_shapes=[pltpu.VMEM((TOP_K, tm, D_MODEL), F32), pltpu.SemaphoreType.DMA(())],
        ),
        out_shape=[
            jax.ShapeDtypeStruct((B, S, D_MODEL), F32),
            jax.ShapeDtypeStruct((B, S, D_MODEL), BF16),
        ],
        compiler_params=_cparams("arbitrary", "arbitrary"),
        name="combine",
    )(dest_flat, x1, gates, ys, mod, modn, ln_w, ln_b)


def _moe(x1, h, rw_pad, rb_pad, ltri, wg, wu, wd, mod, modn, ln_w, ln_b):
    B, S, _ = x1.shape
    n = B * S
    m = n * TOP_K
    rec, cnt = _router(h, rw_pad, rb_pad, ltri)
    rec = rec.reshape(n, LANES)
    e = rec[:, 0:TOP_K].astype(jnp.int32)
    gates = rec[:, TOP_K:2 * TOP_K]
    rank = rec[:, 2 * TOP_K:3 * TOP_K].astype(jnp.int32)
    counts = cnt[0, :N_EXPERTS].astype(jnp.int32)
    padded = (counts + TM_EXP - 1) // TM_EXP * TM_EXP
    pad_end = jnp.cumsum(padded)
    pad_start = pad_end - padded
    dest = (pad_start[e] + rank).reshape(m)
    n_blocks = m // TM_EXP + N_EXPERTS
    block_e = jnp.minimum(
        jnp.searchsorted(pad_end, jnp.arange(n_blocks, dtype=jnp.int32) * TM_EXP, side="right"),
        N_EXPERTS - 1).astype(jnp.int32)
    n_used = (pad_end[-1:] // TM_EXP).astype(jnp.int32)
    slot_tok = jnp.zeros((n_blocks * TM_EXP,), jnp.int32).at[dest].set(
        jnp.arange(m, dtype=jnp.int32) // TOP_K)
    ys = _experts(block_e, n_used, slot_tok, h.reshape(n, D_MODEL), wg, wu, wd)
    return _combine(dest, x1, gates.reshape(B, S, TOP_K), ys, mod, modn, ln_w, ln_b)


def _mla_weights(w_uq, w_ukv):
    L = w_uq.shape[0]
    H = MLA_HEADS
    half = QK_ROPE // 2
    wq = w_uq.reshape(L, Q_LORA, H, QK_NOPE + QK_ROPE)
    nope, rope = wq[..., :QK_NOPE], wq[..., QK_NOPE:]
    zpad = jnp.zeros((L, Q_LORA, H, HEAD_PAD - QK_NOPE - QK_ROPE), w_uq.dtype)
    wq_pad = jnp.concatenate([nope, rope, zpad], -1).reshape(L, Q_LORA, H * HEAD_PAD)
    rope_rot = jnp.concatenate([-rope[..., half:], rope[..., :half]], -1)
    wq_rot = jnp.concatenate([jnp.zeros_like(nope), rope_rot, zpad], -1).reshape(L, Q_LORA, H * HEAD_PAD)
    wkv = w_ukv.reshape(L, KV_LORA, H, QK_NOPE + V_HEAD)
    k_nope, v = wkv[..., :QK_NOPE], wkv[..., QK_NOPE:]
    wk_pad = jnp.concatenate(
        [k_nope, jnp.zeros((L, KV_LORA, H, HEAD_PAD - QK_NOPE), w_ukv.dtype)], -1
    ).reshape(L, KV_LORA, H * HEAD_PAD)
    wv = jnp.swapaxes(v.reshape(L, KV_LORA, MLA_DIM), 1, 2)
    return wq_pad.astype(BF16), wq_rot.astype(BF16), wk_pad.astype(BF16), wv.astype(BF16)


def _rope_placement():
    half = QK_ROPE // 2
    src = jnp.arange(QK_ROPE)[:, None]
    dst = jnp.arange(HEAD_PAD)[None, :]
    place = (dst == src + QK_NOPE).astype(BF16)
    rot = jnp.where((src < half) & (dst == src + QK_NOPE + half), 1.0,
                    jnp.where((src >= half) & (dst == src + QK_NOPE - half), -1.0, 0.0))
    return place, rot.astype(BF16)


def kernel(x, c, positions, w_ada, b_ada, w_in, shift_mu, decay_w0, decay_up, aaa_a0, aaa_up, gate_up, k_k, k_a, r_k, gn_w, gn_b, q_norm_w, w_uq, kv_norm_w, w_ukv, p_a, p_b, w_o, ln1_w, ln1_b, ln2_w, ln2_b, ffn_w_gate, ffn_w_up, ffn_w_down, router_w, router_b, moe_w_gate, moe_w_up, moe_w_down):
    B, S, _ = x.shape
    n = B * S
    L = DEPTH
    assert S % RW_CHUNK == 0 and S % LANES == 0

    mod = _ada(c, w_ada, b_ada).reshape(L, B, 6, D_MODEL)
    posf = positions.astype(F32).reshape(B, S, 1)

    w_shift = w_in[:, :, :SHIFT_DIM].astype(BF16)
    w_lat = jnp.pad(w_in[:, :, SHIFT_DIM:SHIFT_DIM + LAT_DIM],
                    ((0, 0), (0, 0), (0, LAT_PAD - LAT_DIM))).astype(BF16)
    w_gates = w_in[:, :, SHIFT_DIM + LAT_DIM:].astype(BF16)
    wq_pad, wq_rot, wk_pad, wv = _mla_weights(w_uq, w_ukv)
    place, placer = _rope_placement()
    lane_head = jnp.arange(RW_DIM) // RW_HEAD
    hsum = (lane_head[:, None] == lane_head[None, :]).astype(BF16)
    row1 = lambda a, l: a[l].reshape(1, -1)
    tm_r = min(TM_ROUTE, S)
    ltri = (jnp.arange(tm_r)[None, :] < jnp.arange(tm_r)[:, None]).astype(BF16)
    rw_pad = jnp.pad(router_w, ((0, 0), (0, 0), (0, LANES - N_EXPERTS)))
    rb_pad = jnp.pad(router_b, ((0, 0), (0, LANES - N_EXPERTS)), constant_values=NEG_BIG)

    u = _lnmod(x, mod[0])
    for l in range(L):
        u2 = u.reshape(n, D_MODEL)
        ps = _mm(u2, w_shift[l], F32).reshape(B, S, SHIFT_DIM)
        lat = _mm(u2, w_lat[l], F32).reshape(B, S, LAT_PAD)
        sg = _mm(u2, w_gates[l], BF16, act="sigmoid").reshape(B, S, 2 * D_MODEL)
        ya = _rwkv(ps, row1(shift_mu, l), row1(decay_w0, l), decay_up[l].astype(BF16), row1(aaa_a0, l),
                   aaa_up[l].astype(BF16), gate_up[l].astype(BF16), row1(k_k, l), row1(k_a, l),
                   row1(r_k, l), row1(gn_w, l), row1(gn_b, l), hsum)
        q, k, v = _mla_prep(lat, posf, row1(q_norm_w, l), wq_pad[l], wq_rot[l], row1(kv_norm_w, l),
                            wk_pad[l], wv[l], place, placer)
        yb = _attn(q, k, v)
        moe_layer = l % 2 == 1
        x1, h = _merge_out(x, ya, yb, sg, p_a[l].astype(BF16), p_b[l].astype(BF16), w_o[l].astype(BF16),
                           mod[l], row1(ln1_w, l), row1(ln1_b, l), F32 if moe_layer else BF16)
        modn = mod[min(l + 1, L - 1)]
        if moe_layer:
            x, u = _moe(x1, h, rw_pad[l // 2], rb_pad[l // 2].reshape(1, LANES), ltri,
                        moe_w_gate[l // 2].astype(BF16), moe_w_up[l // 2].astype(BF16),
                        moe_w_down[l // 2].astype(BF16), mod[l], modn, row1(ln2_w, l), row1(ln2_b, l))
        else:
            x, u = _ffn(x1, h, ffn_w_gate[l // 2].astype(BF16), ffn_w_up[l // 2].astype(BF16),
                        ffn_w_down[l // 2].astype(BF16), mod[l], modn, row1(ln2_w, l), row1(ln2_b, l))
    return x
```

```python
import functools
import math

import jax
import jax.numpy as jnp
from jax import lax
from jax.experimental import pallas as pl
from jax.experimental.pallas import tpu as pltpu

F32 = jnp.float32
BF16 = jnp.bfloat16

D_MODEL = 1024
DEPTH = 4
RW_HEADS = 8
RW_HEAD = 64
RW_DIM = RW_HEADS * RW_HEAD
DECAY_LORA = 64
AAA_LORA = 64
GATE_LORA = 128
GN_EPS = 64e-5
MLA_HEADS = 8
QK_NOPE = 64
QK_ROPE = 32
V_HEAD = 64
Q_LORA = 384
KV_LORA = 256
MLA_DIM = MLA_HEADS * V_HEAD
ROPE_THETA = 10000.0
SHIFT_DIM = 3 * RW_DIM + DECAY_LORA + AAA_LORA + GATE_LORA
LAT_DIM = Q_LORA + KV_LORA + QK_ROPE
LAT_PAD = 768
D_FF = 2816
N_EXPERTS = 8
TOP_K = 2
D_FF_EXPERT = 3584
ALPHA = (2.0 * DEPTH) ** 0.25
LN_EPS = 1e-5
RMS_EPS = 1e-6
LANES = 128
HEAD_PAD = 128
NEG_BIG = -1e30

RW_CHUNK = 64
RW_SEQS = 4
TM_MM = 512
TM_PREP = 256
TQ = 1024
TM_MERGE = 512
TM_FFN = 512
FF_CHUNK = 256
TM_ROUTE = 512
TM_EXP = 512
EXP_NJ = 2
TM_COMB = 256

_HI = lax.Precision.HIGHEST


def _cparams(*sem):
    return pltpu.CompilerParams(dimension_semantics=sem)


def _dot(a, b):
    return jnp.dot(a.astype(BF16), b.astype(BF16), preferred_element_type=F32)


def _dot_nt(a, b):
    return lax.dot_general(a.astype(BF16), b.astype(BF16), (((1,), (1,)), ((), ())),
                           preferred_element_type=F32)


def _dot_tn(a, b):
    return lax.dot_general(a.astype(BF16), b.astype(BF16), (((0,), (0,)), ((), ())),
                           preferred_element_type=F32)


def _split(x, n):
    parts = []
    for _ in range(n - 1):
        p = x.astype(BF16)
        parts.append(p)
        x = x - p.astype(F32)
    parts.append(x.astype(BF16))
    return parts


def _dot_split_lhs(x, w_bf16, n):
    acc = None
    for p in _split(x, n):
        t = jnp.dot(p, w_bf16, preferred_element_type=F32)
        acc = t if acc is None else acc + t
    return acc


def _dot_split_rhs(w_bf16, x, n):
    acc = None
    for p in _split(x, n):
        t = jnp.dot(w_bf16, p, preferred_element_type=F32)
        acc = t if acc is None else acc + t
    return acc


def _layer_norm(x):
    mu = jnp.mean(x, -1, keepdims=True)
    xc = x - mu
    var = jnp.mean(xc * xc, -1, keepdims=True)
    return xc * lax.rsqrt(var + LN_EPS)


def _full(shape):
    nd = len(shape)
    return pl.BlockSpec(shape, lambda *_: (0,) * nd)


def _ada_kernel(c_ref, w_ref, b_ref, o_ref):
    c = c_ref[...]
    ca = c * jax.nn.sigmoid(c)
    o_ref[0] = jnp.dot(ca, w_ref[0], preferred_element_type=F32, precision=_HI) + b_ref[0]


def _ada(c, w_ada, b_ada):
    L, _, six_d = w_ada.shape
    B = c.shape[0]
    nj = six_d // D_MODEL
    return pl.pallas_call(
        _ada_kernel,
        grid=(L, nj),
        in_specs=[
            pl.BlockSpec((B, D_MODEL), lambda l, j: (0, 0)),
            pl.BlockSpec((1, D_MODEL, D_MODEL), lambda l, j: (l, 0, j)),
            pl.BlockSpec((1, 1, D_MODEL), lambda l, j: (l, 0, j)),
        ],
        out_specs=pl.BlockSpec((1, B, D_MODEL), lambda l, j: (l, 0, j)),
        out_shape=jax.ShapeDtypeStruct((L, B, six_d), F32),
        compiler_params=_cparams("parallel", "parallel"),
        name="ada",
    )(c, w_ada, b_ada.reshape(L, 1, six_d))


def _lnmod_kernel(x_ref, mod_ref, o_ref):
    m = mod_ref[0]
    o_ref[0] = (_layer_norm(x_ref[0]) * (1.0 + m[1:2]) + m[0:1]).astype(o_ref.dtype)


def _lnmod(x, mod):
    B, S, _ = x.shape
    tm = min(TM_MM, S)
    return pl.pallas_call(
        _lnmod_kernel,
        grid=(B, S // tm),
        in_specs=[
            pl.BlockSpec((1, tm, D_MODEL), lambda b, i: (b, i, 0)),
            pl.BlockSpec((1, 6, D_MODEL), lambda b, i: (b, 0, 0)),
        ],
        out_specs=pl.BlockSpec((1, tm, D_MODEL), lambda b, i: (b, i, 0)),
        out_shape=jax.ShapeDtypeStruct((B, S, D_MODEL), BF16),
        compiler_params=_cparams("parallel", "parallel"),
        name="lnmod",
    )(x, mod)


def _mm_kernel(a_ref, w_ref, o_ref, *, act):
    acc = jnp.dot(a_ref[...], w_ref[...], preferred_element_type=F32)
    if act == "sigmoid":
        acc = jax.nn.sigmoid(acc)
    o_ref[...] = acc.astype(o_ref.dtype)


def _mm(a, w, out_dtype, act=None):
    n, k = a.shape
    n_out = w.shape[1]
    tm = min(TM_MM, n)
    return pl.pallas_call(
        functools.partial(_mm_kernel, act=act),
        grid=(n // tm,),
        in_specs=[
            pl.BlockSpec((tm, k), lambda i: (i, 0)),
            pl.BlockSpec((k, n_out), lambda i: (0, 0)),
        ],
        out_specs=pl.BlockSpec((tm, n_out), lambda i: (i, 0)),
        out_shape=jax.ShapeDtypeStruct((n, n_out), out_dtype),
        compiler_params=_cparams("parallel"),
        name="mm",
    )(a, w)


def _rwkv_kernel(ps_ref, mu_ref, w0_ref, dup_ref, a0_ref, aup_ref, gup_ref, kk_ref, ka_ref, rk_ref,
                 gnw_ref, gnb_ref, hsum_ref, o_ref, state_ref, prev_ref, y_ref):
    C = RW_CHUNK
    N = RW_HEAD
    G = ps_ref.shape[0]

    @pl.when(pl.program_id(1) == 0)
    def _init():
        state_ref[...] = jnp.zeros_like(state_ref)
        prev_ref[...] = jnp.zeros_like(prev_ref)

    half = RW_DIM // 2
    hs_lo = hsum_ref[0:half, 0:half]
    hs_hi = hsum_ref[half:, half:]

    def head_sum(z):
        zb = z.astype(BF16)
        return jnp.concatenate([jnp.dot(zb[:, 0:half], hs_lo, preferred_element_type=F32),
                                jnp.dot(zb[:, half:], hs_hi, preferred_element_type=F32)], axis=1)

    ri = lax.broadcasted_iota(jnp.int32, (C, C), 0)
    ci = lax.broadcasted_iota(jnp.int32, (C, C), 1)
    tri = (ci <= ri).astype(BF16)
    r4 = lax.broadcasted_iota(jnp.int32, (2 * C, 4 * C), 0)
    c4 = lax.broadcasted_iota(jnp.int32, (2 * C, 4 * C), 1)
    keep = (c4 % C) < (r4 % C) + (r4 >= C).astype(jnp.int32)
    r1 = lax.broadcasted_iota(jnp.int32, (C, 2 * C), 0)
    c1 = lax.broadcasted_iota(jnp.int32, (C, 2 * C), 1)
    low = c1 < C
    eye_hi = (c1 == r1 + C).astype(F32)

    prep = []
    for s in range(G):
        p = ps_ref[s]
        row = lax.broadcasted_iota(jnp.int32, p.shape, 0)
        shifted = jnp.where(row == 0, prev_ref[s, 0:1, :], pltpu.roll(p, 1, 0))
        prev_ref[s, 0:1, :] = p[C - 1:C, :]
        xs = p + (shifted - p) * mu_ref[...]
        r = xs[:, 0:RW_DIM]
        k = xs[:, RW_DIM:2 * RW_DIM]
        v = xs[:, 2 * RW_DIM:3 * RW_DIM]
        o = 3 * RW_DIM
        dw = xs[:, o:o + DECAY_LORA]
        da = xs[:, o + DECAY_LORA:o + DECAY_LORA + AAA_LORA]
        dg = xs[:, o + DECAY_LORA + AAA_LORA:SHIFT_DIM]
        lw = -math.exp(-0.5) * jax.nn.sigmoid(w0_ref[...] + _dot(jnp.tanh(dw), dup_ref[...]))
        a = jax.nn.sigmoid(a0_ref[...] + _dot(da, aup_ref[...]))
        gate = _dot(jax.nn.sigmoid(dg), gup_ref[...])
        kk = k * kk_ref[...]
        kk = kk * lax.rsqrt(jnp.maximum(head_sum(kk * kk), 1e-24))
        k_mod = k * (1.0 + (a - 1.0) * ka_ref[...])
        b = kk * a
        bonus = head_sum(r * k_mod * rk_ref[...]) * v
        cw = _dot_split_rhs(tri, lw, 2)
        cw_last = cw[C - 1:C, :]
        e_neg = jnp.exp(-cw)
        e_rem = jnp.exp(cw_last - cw)
        prep.append(dict(
            kp=(kk * jnp.exp(cw - lw)).astype(BF16), rp=(r * jnp.exp(cw)).astype(BF16),
            bm=(b * e_neg).astype(BF16), km=(k_mod * e_neg).astype(BF16),
            bt=(b * e_rem).astype(BF16), kt=(k_mod * e_rem).astype(BF16),
            v=v.astype(BF16), w_c=jnp.exp(cw_last), bonus=bonus, gate=gate))

    chains = [(s, h) for s in range(G) for h in range(RW_HEADS)]
    sl = lambda h: slice(h * N, (h + 1) * N)

    krs, aas = [], []
    for s, h in chains:
        d = prep[s]
        kr = jnp.concatenate([d["kp"][:, sl(h)], d["rp"][:, sl(h)]], axis=0)
        bk = jnp.concatenate([d["bm"][:, sl(h)], d["km"][:, sl(h)], d["km"][:, sl(h)], d["bm"][:, sl(h)]],
                             axis=0)
        krs.append(kr)
        aas.append(jnp.where(keep, _dot_nt(kr, bk), 0.0))
    s0s = [state_ref[s, h] for s, h in chains]
    xxs = [_dot_nt(kr, s0) for kr, s0 in zip(krs, s0s)]
    rts = [jnp.where(low, _dot(aa[0:C, 0:C], aa[0:C, 0:2 * C]), eye_hi - aa[0:C, 2 * C:4 * C]) for aa in aas]
    for _ in range(int(math.log2(C)) - 1):
        outs = [_dot(rt[:, 0:C], rt) for rt in rts]
        rts = [jnp.where(low, out, rt + out) for rt, out in zip(rts, outs)]
    vs = [prep[s]["v"][:, sl(h)] for s, h in chains]
    gms = [xx[0:C] + _dot(aa[0:C, 2 * C:3 * C], v_h) for xx, aa, v_h in zip(xxs, aas, vs)]
    us = [-_dot(rt[:, C:2 * C], gm) for rt, gm in zip(rts, gms)]
    uvs = [jnp.concatenate([u.astype(BF16), v_h], axis=0) for u, v_h in zip(us, vs)]
    ys = [xx[C:2 * C] + _dot(aa[C:2 * C, 0:2 * C], uv) for xx, aa, uv in zip(xxs, aas, uvs)]
    news = []
    for (s, h), s0, uv in zip(chains, s0s, uvs):
        d = prep[s]
        btkt = jnp.concatenate([d["bt"][:, sl(h)], d["kt"][:, sl(h)]], axis=0)
        news.append(s0 * d["w_c"][:, sl(h)] + _dot_tn(uv, btkt))
    for (s, h), y_h, new in zip(chains, ys, news):
        y_ref[s, :, sl(h)] = y_h
        state_ref[s, h] = new

    for s in range(G):
        d = prep[s]
        y = y_ref[s]
        mean = head_sum(y) * (1.0 / N)
        yc = y - mean
        var = head_sum(yc * yc) * (1.0 / N)
        yn = yc * lax.rsqrt(var + GN_EPS) * gnw_ref[...] + gnb_ref[...]
        o_ref[s] = ((yn + d["bonus"]) * d["gate"]).astype(o_ref.dtype)


def _rwkv(ps, mu, w0, dup, a0, aup, gup, k_k, k_a, r_k, gn_w, gn_b, hsum):
    B, S, _ = ps.shape
    C = RW_CHUNK
    G = RW_SEQS if B % RW_SEQS == 0 else 1
    row = lambda n: pl.BlockSpec((1, n), lambda b, j: (0, 0))
    return pl.pallas_call(
        _rwkv_kernel,
        grid=(B // G, S // C),
        in_specs=[
            pl.BlockSpec((G, C, SHIFT_DIM), lambda b, j: (b, j, 0)),
            row(SHIFT_DIM), row(RW_DIM),
            pl.BlockSpec((DECAY_LORA, RW_DIM), lambda b, j: (0, 0)),
            row(RW_DIM),
            pl.BlockSpec((AAA_LORA, RW_DIM), lambda b, j: (0, 0)),
            pl.BlockSpec((GATE_LORA, RW_DIM), lambda b, j: (0, 0)),
            row(RW_DIM), row(RW_DIM), row(RW_DIM), row(RW_DIM), row(RW_DIM),
            pl.BlockSpec((RW_DIM, RW_DIM), lambda b, j: (0, 0)),
        ],
        out_specs=pl.BlockSpec((G, C, RW_DIM), lambda b, j: (b, j, 0)),
        out_shape=jax.ShapeDtypeStruct((B, S, RW_DIM), BF16),
        scratch_shapes=[
            pltpu.VMEM((G, RW_HEADS, RW_HEAD, RW_HEAD), F32),
            pltpu.VMEM((G, 8, SHIFT_DIM), F32),
            pltpu.VMEM((G, C, RW_DIM), F32),
        ],
        compiler_params=_cparams("parallel", "arbitrary"),
        name="rwkv",
    )(ps, mu, w0, dup, a0, aup, gup, k_k, k_a, r_k, gn_w, gn_b, hsum)


def _rope_kernel(pos_ref, cos_ref, sin_ref):
    half = QK_ROPE // 2
    lane = lax.broadcasted_iota(jnp.int32, (1, HEAD_PAD), 1)
    in_rope = (lane >= QK_NOPE) & (lane < QK_NOPE + QK_ROPE)
    fidx = jnp.where(lane < QK_NOPE + half, lane - QK_NOPE, lane - QK_NOPE - half)
    inv_freq = jnp.exp(-math.log(ROPE_THETA) * fidx.astype(F32) / half)
    ang = pos_ref[0] * inv_freq
    cos_ref[0] = jnp.where(in_rope, jnp.cos(ang), 0.0)
    sin_ref[0] = jnp.where(in_rope, jnp.sin(ang), 0.0)


def _rope_tables(posf):
    B, S, _ = posf.shape
    tm = min(TM_PREP, S)
    out = pl.BlockSpec((1, tm, HEAD_PAD), lambda b, i: (b, i, 0))
    return pl.pallas_call(
        _rope_kernel,
        grid=(B, S // tm),
        in_specs=[pl.BlockSpec((1, tm, 1), lambda b, i: (b, i, 0))],
        out_specs=[out, out],
        out_shape=[jax.ShapeDtypeStruct((B, S, HEAD_PAD), F32)] * 2,
        compiler_params=_cparams("parallel", "parallel"),
        name="rope_tables",
    )(posf)


def _mla_prep_kernel(lat_ref, cos_ref, sin_ref, qnw_ref, wq_ref, wqr_ref, kvnw_ref, wk_ref, wv_ref,
                     place_ref, placer_ref, q_ref, k_ref, v_ref):
    lat = lat_ref[0]
    c_q = lat[:, 0:Q_LORA]
    c_kv = lat[:, Q_LORA:Q_LORA + KV_LORA]
    k_rope = lat[:, Q_LORA + KV_LORA:LAT_DIM]

    def rms(z, w):
        return z * lax.rsqrt(jnp.mean(z * z, -1, keepdims=True) + RMS_EPS) * w

    lane = lax.broadcasted_iota(jnp.int32, (1, HEAD_PAD), 1)
    cos_r = cos_ref[0]
    sin_r = sin_ref[0]
    scale = (QK_NOPE + QK_ROPE) ** -0.5
    q_mul = jnp.where(lane < QK_NOPE, 1.0, cos_r) * scale
    q_mul_rot = sin_r * scale

    cq_n = rms(c_q, qnw_ref[...]).astype(BF16)
    q_all = jnp.dot(cq_n, wq_ref[...], preferred_element_type=F32)
    q_rot = jnp.dot(cq_n, wqr_ref[...], preferred_element_type=F32)
    ckv_n = rms(c_kv, kvnw_ref[...]).astype(BF16)
    k_all = jnp.dot(ckv_n, wk_ref[...], preferred_element_type=F32)
    v_all_t = lax.dot_general(wv_ref[...], ckv_n, (((1,), (1,)), ((), ())), preferred_element_type=F32)
    kr_pad = (_dot_split_lhs(k_rope, place_ref[...], 3) * cos_r
              + _dot_split_lhs(k_rope, placer_ref[...], 3) * sin_r)
    for h in range(MLA_HEADS):
        sl = slice(h * HEAD_PAD, (h + 1) * HEAD_PAD)
        q_ref[0, h] = (q_all[:, sl] * q_mul + q_rot[:, sl] * q_mul_rot).astype(q_ref.dtype)
        k_ref[0, h] = (k_all[:, sl] + kr_pad).astype(k_ref.dtype)
    for hp in range(MLA_HEADS // 2):
        v_ref[0, hp] = v_all_t[hp * LANES:(hp + 1) * LANES, :].astype(v_ref.dtype)


def _mla_prep(lat, cos_t, sin_t, qnw, wq, wqr, kvnw, wk, wv, place, placer):
    B, S, _ = lat.shape
    tm = min(TM_PREP, S)
    H = MLA_HEADS
    cst = lambda shape: pl.BlockSpec(shape, lambda b, i: (0,) * len(shape))
    return pl.pallas_call(
        _mla_prep_kernel,
        grid=(B, S // tm),
        in_specs=[
            pl.BlockSpec((1, tm, LAT_PAD), lambda b, i: (b, i, 0)),
            pl.BlockSpec((1, tm, HEAD_PAD), lambda b, i: (b, i, 0)),
            pl.BlockSpec((1, tm, HEAD_PAD), lambda b, i: (b, i, 0)),
            cst((1, Q_LORA)), cst((Q_LORA, H * HEAD_PAD)), cst((Q_LORA, H * HEAD_PAD)),
            cst((1, KV_LORA)), cst((KV_LORA, H * HEAD_PAD)), cst((MLA_DIM, KV_LORA)),
            cst((QK_ROPE, HEAD_PAD)), cst((QK_ROPE, HEAD_PAD)),
        ],
        out_specs=[
            pl.BlockSpec((1, H, tm, HEAD_PAD), lambda b, i: (b, 0, i, 0)),
            pl.BlockSpec((1, H, tm, HEAD_PAD), lambda b, i: (b, 0, i, 0)),
            pl.BlockSpec((1, H // 2, LANES, tm), lambda b, i: (b, 0, 0, i)),
        ],
        out_shape=[
            jax.ShapeDtypeStruct((B, H, S, HEAD_PAD), BF16),
            jax.ShapeDtypeStruct((B, H, S, HEAD_PAD), BF16),
            jax.ShapeDtypeStruct((B, H // 2, LANES, S), BF16),
        ],
        compiler_params=_cparams("parallel", "parallel"),
        name="mla_prep",
    )(lat, cos_t, sin_t, qnw, wq, wqr, kvnw, wk, wv, place, placer)


def _attn_kernel(q_ref, k_ref, vt_ref, o_ref, *, tq):
    i = pl.program_id(2)
    ki = lax.broadcasted_iota(jnp.int32, (tq, tq), 0)
    qi = lax.broadcasted_iota(jnp.int32, (tq, tq), 1)
    row_v = lax.broadcasted_iota(jnp.int32, (LANES, tq), 0)
    one = jnp.ones((), BF16)
    qs = [q_ref[0, hh] for hh in range(2)]

    def step(j, carry, masked):
        start = pl.multiple_of(j * tq, tq)
        vt = vt_ref[0, 0, :, pl.ds(start, tq)]
        v_aug = [jnp.where(row_v < V_HEAD, vt, one), jnp.where(row_v < V_HEAD, one, vt)]
        ss = [lax.dot_general(k_ref[0, hh, pl.ds(start, tq), :], qs[hh], (((1,), (1,)), ((), ())),
                              preferred_element_type=F32) for hh in range(2)]
        new = []
        ps = []
        for hh in range(2):
            m, acc = carry[hh]
            s = ss[hh]
            if masked:
                s = jnp.where(ki <= qi, s, NEG_BIG)
            m_new = jnp.maximum(m, jnp.max(s, 0, keepdims=True))
            ps.append(jnp.exp(s - m_new).astype(BF16))
            new.append((m_new, jnp.exp(m - m_new) * acc))
        pvs = [jnp.dot(v_aug[hh], ps[hh], preferred_element_type=F32) for hh in range(2)]
        return tuple((new[hh][0], new[hh][1] + pvs[hh]) for hh in range(2))

    init1 = (jnp.full((1, tq), NEG_BIG, F32), jnp.zeros((LANES, tq), F32))
    carry = lax.fori_loop(0, i, functools.partial(step, masked=False), (init1, init1))
    carry = step(i, carry, True)
    a0 = carry[0][1]
    a1 = carry[1][1]
    out_t = jnp.concatenate([a0[0:V_HEAD] / a0[V_HEAD:], a1[V_HEAD:] / a1[0:V_HEAD]], axis=0)
    o_ref[0] = out_t.T.astype(o_ref.dtype)


def _attn(q, k, vt):
    B, H, S, _ = q.shape
    tq = min(TQ, S)
    return pl.pallas_call(
        functools.partial(_attn_kernel, tq=tq),
        grid=(B, H // 2, S // tq),
        in_specs=[
            pl.BlockSpec((1, 2, tq, HEAD_PAD), lambda b, p, i: (b, p, i, 0)),
            pl.BlockSpec((1, 2, S, HEAD_PAD), lambda b, p, i: (b, p, 0, 0)),
            pl.BlockSpec((1, 1, LANES, S), lambda b, p, i: (b, p, 0, 0)),
        ],
        out_specs=pl.BlockSpec((1, tq, LANES), lambda b, p, i: (b, i, p)),
        out_shape=jax.ShapeDtypeStruct((B, S, MLA_DIM), BF16),
        compiler_params=_cparams("parallel", "parallel", "arbitrary"),
        name="attn",
    )(q, k, vt)


def _residual_epilogue(x, f, gate, ln_w, ln_b):
    return _layer_norm(ALPHA * x + gate * f) * ln_w + ln_b


def _merge_out_kernel(x_ref, ya_ref, yb_ref, sga_ref, sgb_ref, pa_ref, pb_ref, wo_ref, mod_ref,
                      lnw_ref, lnb_ref, x1_ref, h_ref):
    m = mod_ref[0]
    tm = x_ref.shape[1]
    halves = [slice(0, tm // 2), slice(tm // 2, tm)]
    mas = [jnp.dot(ya_ref[0, r, :], pa_ref[...], preferred_element_type=F32) for r in halves]
    mbs = [jnp.dot(yb_ref[0, r, :], pb_ref[...], preferred_element_type=F32) for r in halves]
    merged = [(sga_ref[0, r, :].astype(F32) * ma + sgb_ref[0, r, :].astype(F32) * mb).astype(BF16)
              for r, ma, mb in zip(halves, mas, mbs)]
    mixes = [jnp.dot(mg, wo_ref[...], preferred_element_type=F32) for mg in merged]
    for r, mix in zip(halves, mixes):
        x1 = _residual_epilogue(x_ref[0, r, :], mix, m[2:3], lnw_ref[...], lnb_ref[...])
        x1_ref[0, r, :] = x1
        h_ref[0, r, :] = (_layer_norm(x1) * (1.0 + m[4:5]) + m[3:4]).astype(h_ref.dtype)


def _merge_out(x, ya, yb, sg, pa, pb, wo, mod, ln_w, ln_b, h_dtype):
    B, S, _ = x.shape
    tm = min(TM_MERGE, S)
    cst = lambda shape: pl.BlockSpec(shape, lambda b, i: (0,) * len(shape))
    rows = lambda n: pl.BlockSpec((1, tm, n), lambda b, i: (b, i, 0))
    return pl.pallas_call(
        _merge_out_kernel,
        grid=(B, S // tm),
        in_specs=[
            rows(D_MODEL), rows(RW_DIM), rows(MLA_DIM),
            pl.BlockSpec((1, tm, D_MODEL), lambda b, i: (b, i, 0)),
            pl.BlockSpec((1, tm, D_MODEL), lambda b, i: (b, i, 1)),
            cst((RW_DIM, D_MODEL)), cst((MLA_DIM, D_MODEL)), cst((D_MODEL, D_MODEL)),
            pl.BlockSpec((1, 6, D_MODEL), lambda b, i: (b, 0, 0)),
            cst((1, D_MODEL)), cst((1, D_MODEL)),
        ],
        out_specs=[rows(D_MODEL), rows(D_MODEL)],
        out_shape=[
            jax.ShapeDtypeStruct((B, S, D_MODEL), F32),
            jax.ShapeDtypeStruct((B, S, D_MODEL), h_dtype),
        ],
        compiler_params=_cparams("parallel", "parallel"),
        name="merge_out",
    )(x, ya, yb, sg, sg, pa, pb, wo, mod, ln_w, ln_b)


def _write_epilogue(x1, f, m, modn_ref, lnw_ref, lnb_ref, x2_ref, u_ref):
    x2 = _residual_epilogue(x1, f, m[5:6], lnw_ref[...], lnb_ref[...])
    x2_ref[0] = x2
    mn = modn_ref[0]
    u_ref[0] = (_layer_norm(x2) * (1.0 + mn[1:2]) + mn[0:1]).astype(u_ref.dtype)


def _ffn_kernel(x1_ref, h_ref, wg_ref, wu_ref, wd_ref, mod_ref, modn_ref, lnw_ref, lnb_ref,
                x2_ref, u_ref):
    h = h_ref[0]
    acc = None
    for j in range(D_FF // FF_CHUNK):
        sl = slice(j * FF_CHUNK, (j + 1) * FF_CHUNK)
        gt = jnp.dot(h, wg_ref[:, sl], preferred_element_type=F32)
        up = jnp.dot(h, wu_ref[:, sl], preferred_element_type=F32)
        act = (gt * jax.nn.sigmoid(gt) * up).astype(BF16)
        t = jnp.dot(act, wd_ref[sl, :], preferred_element_type=F32)
        acc = t if acc is None else acc + t
    _write_epilogue(x1_ref[0], acc, mod_ref[0], modn_ref, lnw_ref, lnb_ref, x2_ref, u_ref)


def _ffn(x1, h, wg, wu, wd, mod, modn, ln_w, ln_b):
    B, S, _ = x1.shape
    tm = min(TM_FFN, S)
    res = lambda shape: pl.BlockSpec(shape, lambda b, i: (0,) * len(shape), pipeline_mode=pl.Buffered(1))
    cst = lambda shape: pl.BlockSpec(shape, lambda b, i: (0,) * len(shape))
    rows = pl.BlockSpec((1, tm, D_MODEL), lambda b, i: (b, i, 0))
    modspec = pl.BlockSpec((1, 6, D_MODEL), lambda b, i: (b, 0, 0))
    return pl.pallas_call(
        _ffn_kernel,
        grid=(B, S // tm),
        in_specs=[rows, rows, res((D_MODEL, D_FF)), res((D_MODEL, D_FF)), res((D_FF, D_MODEL)),
                  modspec, modspec, cst((1, D_MODEL)), cst((1, D_MODEL))],
        out_specs=[rows, rows],
        out_shape=[
            jax.ShapeDtypeStruct((B, S, D_MODEL), F32),
            jax.ShapeDtypeStruct((B, S, D_MODEL), BF16),
        ],
        compiler_params=_cparams("parallel", "parallel"),
        name="ffn",
    )(x1, h, wg, wu, wd, mod, modn, ln_w, ln_b)


def _router_kernel(h_ref, rw_ref, rb_ref, ltri_ref, rec_ref, cnt_ref, carry_ref):
    first = (pl.program_id(0) == 0) & (pl.program_id(1) == 0)

    @pl.when(first)
    def _init():
        carry_ref[...] = jnp.zeros_like(carry_ref)

    h = h_ref[0]
    tm = h.shape[0]
    logits = jnp.dot(h, rw_ref[...], preferred_element_type=F32, precision=_HI) + rb_ref[...]
    lane = lax.broadcasted_iota(jnp.int32, (tm, LANES), 1)
    m1 = jnp.max(logits, -1, keepdims=True)
    i1 = jnp.min(jnp.where(logits == m1, lane, LANES), -1, keepdims=True)
    rest = jnp.where(lane == i1, -jnp.inf, logits)
    m2 = jnp.max(rest, -1, keepdims=True)
    i2 = jnp.min(jnp.where(rest == m2, lane, LANES), -1, keepdims=True)
    e21 = jnp.exp(m2 - m1)
    g1 = 1.0 / (1.0 + e21)
    g2 = e21 / (1.0 + e21)
    oh1 = lane == i1
    oh2 = lane == i2
    cnt = oh1.astype(F32) + oh2.astype(F32)
    before = jnp.dot(ltri_ref[...], cnt.astype(BF16), preferred_element_type=F32) + carry_ref[0:1, :]
    rank1 = jnp.sum(jnp.where(oh1, before, 0.0), -1, keepdims=True)
    rank2 = jnp.sum(jnp.where(oh2, before, 0.0), -1, keepdims=True)
    carry_ref[0:1, :] = carry_ref[0:1, :] + jnp.sum(cnt, 0, keepdims=True)
    cols = (i1.astype(F32), i2.astype(F32), g1, g2, rank1, rank2)
    rec = jnp.zeros((tm, LANES), F32)
    for c, val in enumerate(cols):
        rec = jnp.where(lane == c, val, rec)
    rec_ref[0] = rec
    cnt_ref[...] = carry_ref[...]


def _router(h, rw_pad, rb_pad, ltri):
    B, S, _ = h.shape
    tm = min(TM_ROUTE, S)
    return pl.pallas_call(
        _router_kernel,
        grid=(B, S // tm),
        in_specs=[
            pl.BlockSpec((1, tm, D_MODEL), lambda b, i: (b, i, 0)),
            pl.BlockSpec((D_MODEL, LANES), lambda b, i: (0, 0)),
            pl.BlockSpec((1, LANES), lambda b, i: (0, 0)),
            pl.BlockSpec((tm, tm), lambda b, i: (0, 0)),
        ],
        out_specs=[
            pl.BlockSpec((1, tm, LANES), lambda b, i: (b, i, 0)),
            pl.BlockSpec((8, LANES), lambda b, i: (0, 0)),
        ],
        out_shape=[
            jax.ShapeDtypeStruct((B, S, LANES), F32),
            jax.ShapeDtypeStruct((8, LANES), F32),
        ],
        scratch_shapes=[pltpu.VMEM((8, LANES), F32)],
        compiler_params=_cparams("arbitrary", "arbitrary"),
        name="router",
    )(h, rw_pad, rb_pad, ltri)


def _expert_kernel(be_ref, nu_ref, tok_ref, h_hbm, wg_ref, wu_ref, wd_ref, ys_ref,
                   rows_ref, x_ref, sems, *, tm):
    i = pl.program_id(0)
    j = pl.program_id(1)
    nb = pl.num_programs(0)
    nj = pl.num_programs(1)
    per_step = tm // EXP_NJ
    used = i < nu_ref[0]

    def row(block, slot, r):
        return pltpu.make_async_copy(h_hbm.at[pl.ds(tok_ref[block * tm + r], 1)],
                                     rows_ref.at[slot, pl.ds(r, 1)], sems.at[slot])

    def wait_block(slot):
        def wait(r, _):
            row(0, slot, r).wait()
            return 0
        lax.fori_loop(0, tm, wait, 0, unroll=8)

    @pl.when((i == 0) & (j == 0))
    def _first():
        def start(r, _):
            row(0, 0, r).start()
            return 0
        lax.fori_loop(0, tm, start, 0, unroll=8)

    @pl.when(j == 0)
    def _arrive():
        wait_block(i % 2)
        x_ref[...] = rows_ref[i % 2].astype(BF16)

    nxt = jnp.minimum(i + 1, nb - 1)
    for r in range(per_step):
        row(nxt, (i + 1) % 2, j * per_step + r).start()

    x = x_ref[...]
    acc = None
    for c in range(wg_ref.shape[2] // FF_CHUNK):
        cols = slice(c * FF_CHUNK, (c + 1) * FF_CHUNK)
        gt = jnp.dot(x, wg_ref[0, :, cols], preferred_element_type=F32)
        up = jnp.dot(x, wu_ref[0, :, cols], preferred_element_type=F32)
        act = (gt * jax.nn.sigmoid(gt) * up).astype(BF16)
        t = jnp.dot(act, wd_ref[0, cols, :], preferred_element_type=F32)
        acc = t if acc is None else acc + t
    acc = jnp.where(used, acc, 0.0)

    @pl.when(j == 0)
    def _():
        ys_ref[...] = acc

    @pl.when(j > 0)
    def _():
        ys_ref[...] += acc

    @pl.when((i == nb - 1) & (j == nj - 1))
    def _drain():
        wait_block((i + 1) % 2)


def _experts(block_e, n_used, slot_tok, h2d, wg, wu, wd):
    m_pad = slot_tok.shape[0]
    tm = TM_EXP
    nj = EXP_NJ
    tf = D_FF_EXPERT // nj

    def jj(i, j, nu):
        return jnp.where(i < nu[0], j, nj - 1)

    return pl.pallas_call(
        functools.partial(_expert_kernel, tm=tm),
        grid_spec=pltpu.PrefetchScalarGridSpec(
            num_scalar_prefetch=3,
            grid=(m_pad // tm, nj),
            in_specs=[
                pl.BlockSpec(memory_space=pl.ANY),
                pl.BlockSpec((1, D_MODEL, tf), lambda i, j, be, nu, tok: (be[i], 0, jj(i, j, nu))),
                pl.BlockSpec((1, D_MODEL, tf), lambda i, j, be, nu, tok: (be[i], 0, jj(i, j, nu))),
                pl.BlockSpec((1, tf, D_MODEL), lambda i, j, be, nu, tok: (be[i], jj(i, j, nu), 0)),
            ],
            out_specs=pl.BlockSpec((tm, D_MODEL), lambda i, j, be, nu, tok: (i, 0)),
            scratch_shapes=[pltpu.VMEM((2, tm, D_MODEL), F32), pltpu.VMEM((tm, D_MODEL), BF16),
                            pltpu.SemaphoreType.DMA((2,))],
        ),
        out_shape=jax.ShapeDtypeStruct((m_pad, D_MODEL), F32),
        compiler_params=_cparams("arbitrary", "arbitrary"),
        name="experts",
    )(block_e, n_used, slot_tok, h2d, wg, wu, wd)


def _combine_kernel(dest_ref, x1_ref, gate_ref, ys_hbm, mod_ref, modn_ref, lnw_ref, lnb_ref,
                    x2_ref, u_ref, buf_ref, sem, *, tm, s_len):
    base = pl.program_id(0) * s_len + pl.program_id(1) * tm

    def row(r, k):
        t = base + r
        return pltpu.make_async_copy(ys_hbm.at[pl.ds(dest_ref[2 * t + k], 1)],
                                     buf_ref.at[k, pl.ds(r, 1)], sem)

    def start(r, _):
        row(r, 0).start()
        row(r, 1).start()
        return 0

    lax.fori_loop(0, tm, start, 0, unroll=8)

    def wait(r, _):
        row(r, 0).wait()
        row(r, 1).wait()
        return 0

    lax.fori_loop(0, tm, wait, 0, unroll=8)
    gate = gate_ref[0]
    f = gate[:, 0:1] * buf_ref[0] + gate[:, 1:2] * buf_ref[1]
    _write_epilogue(x1_ref[0], f, mod_ref[0], modn_ref, lnw_ref, lnb_ref, x2_ref, u_ref)


def _combine(dest_flat, x1, gates, ys, mod, modn, ln_w, ln_b):
    B, S, _ = x1.shape
    tm = min(TM_COMB, S)
    rows = pl.BlockSpec((1, tm, D_MODEL), lambda b, i, d: (b, i, 0))
    modspec = pl.BlockSpec((1, 6, D_MODEL), lambda b, i, d: (b, 0, 0))
    cst = pl.BlockSpec((1, D_MODEL), lambda b, i, d: (0, 0))
    return pl.pallas_call(
        functools.partial(_combine_kernel, tm=tm, s_len=S),
        grid_spec=pltpu.PrefetchScalarGridSpec(
            num_scalar_prefetch=1,
            grid=(B, S // tm),
            in_specs=[rows, pl.BlockSpec((1, tm, TOP_K), lambda b, i, d: (b, i, 0)),
                      pl.BlockSpec(memory_space=pl.ANY), modspec, modspec, cst, cst],
            out_specs=[rows, rows],
            scratch_shapes=[pltpu.VMEM((TOP_K, tm, D_MODEL), F32), pltpu.SemaphoreType.DMA(())],
        ),
        out_shape=[
            jax.ShapeDtypeStruct((B, S, D_MODEL), F32),
            jax.ShapeDtypeStruct((B, S, D_MODEL), BF16),
        ],
        compiler_params=_cparams("arbitrary", "arbitrary"),
        name="combine",
    )(dest_flat, x1, gates, ys, mod, modn, ln_w, ln_b)


def _moe(x1, h, rw_pad, rb_pad, ltri, wg, wu, wd, mod, modn, ln_w, ln_b):
    B, S, _ = x1.shape
    n = B * S
    m = n * TOP_K
    rec, cnt = _router(h, rw_pad, rb_pad, ltri)
    rec = rec.reshape(n, LANES)
    e = rec[:, 0:TOP_K].astype(jnp.int32)
    gates = rec[:, TOP_K:2 * TOP_K]
    rank = rec[:, 2 * TOP_K:3 * TOP_K].astype(jnp.int32)
    counts = cnt[0, :N_EXPERTS].astype(jnp.int32)
    padded = (counts + TM_EXP - 1) // TM_EXP * TM_EXP
    pad_end = jnp.cumsum(padded)
    pad_start = pad_end - padded
    dest = (pad_start[e] + rank).reshape(m)
    n_blocks = m // TM_EXP + N_EXPERTS
    block_e = jnp.minimum(
        jnp.searchsorted(pad_end, jnp.arange(n_blocks, dtype=jnp.int32) * TM_EXP, side="right"),
        N_EXPERTS - 1).astype(jnp.int32)
    n_used = (pad_end[-1:] // TM_EXP).astype(jnp.int32)
    slot_tok = jnp.zeros((n_blocks * TM_EXP,), jnp.int32).at[dest].set(
        jnp.arange(m, dtype=jnp.int32) // TOP_K)
    ys = _experts(block_e, n_used, slot_tok, h.reshape(n, D_MODEL), wg, wu, wd)
    return _combine(dest, x1, gates.reshape(B, S, TOP_K), ys, mod, modn, ln_w, ln_b)


def _mla_weights(w_uq, w_ukv):
    L = w_uq.shape[0]
    H = MLA_HEADS
    half = QK_ROPE // 2
    wq = w_uq.reshape(L, Q_LORA, H, QK_NOPE + QK_ROPE)
    nope, rope = wq[..., :QK_NOPE], wq[..., QK_NOPE:]
    zpad = jnp.zeros((L, Q_LORA, H, HEAD_PAD - QK_NOPE - QK_ROPE), w_uq.dtype)
    wq_pad = jnp.concatenate([nope, rope, zpad], -1).reshape(L, Q_LORA, H * HEAD_PAD)
    rope_rot = jnp.concatenate([-rope[..., half:], rope[..., :half]], -1)
    wq_rot = jnp.concatenate([jnp.zeros_like(nope), rope_rot, zpad], -1).reshape(L, Q_LORA, H * HEAD_PAD)
    wkv = w_ukv.reshape(L, KV_LORA, H, QK_NOPE + V_HEAD)
    k_nope, v = wkv[..., :QK_NOPE], wkv[..., QK_NOPE:]
    wk_pad = jnp.concatenate(
        [k_nope, jnp.zeros((L, KV_LORA, H, HEAD_PAD - QK_NOPE), w_ukv.dtype)], -1
    ).reshape(L, KV_LORA, H * HEAD_PAD)
    wv = jnp.swapaxes(v.reshape(L, KV_LORA, MLA_DIM), 1, 2)
    return wq_pad.astype(BF16), wq_rot.astype(BF16), wk_pad.astype(BF16), wv.astype(BF16)


def _rope_placement():
    half = QK_ROPE // 2
    src = jnp.arange(QK_ROPE)[:, None]
    dst = jnp.arange(HEAD_PAD)[None, :]
    place = (dst == src + QK_NOPE).astype(BF16)
    rot = jnp.where((src < half) & (dst == src + QK_NOPE + half), 1.0,
                    jnp.where((src >= half) & (dst == src + QK_NOPE - half), -1.0, 0.0))
    return place, rot.astype(BF16)


def kernel(x, c, positions, w_ada, b_ada, w_in, shift_mu, decay_w0, decay_up, aaa_a0, aaa_up, gate_up, k_k, k_a, r_k, gn_w, gn_b, q_norm_w, w_uq, kv_norm_w, w_ukv, p_a, p_b, w_o, ln1_w, ln1_b, ln2_w, ln2_b, ffn_w_gate, ffn_w_up, ffn_w_down, router_w, router_b, moe_w_gate, moe_w_up, moe_w_down):
    B, S, _ = x.shape
    n = B * S
    L = DEPTH
    assert S % RW_CHUNK == 0 and S % LANES == 0

    mod = _ada(c, w_ada, b_ada).reshape(L, B, 6, D_MODEL)
    cos_t, sin_t = _rope_tables(positions.astype(F32).reshape(B, S, 1))

    w_shift = w_in[:, :, :SHIFT_DIM].astype(BF16)
    w_lat = jnp.pad(w_in[:, :, SHIFT_DIM:SHIFT_DIM + LAT_DIM],
                    ((0, 0), (0, 0), (0, LAT_PAD - LAT_DIM))).astype(BF16)
    w_gates = w_in[:, :, SHIFT_DIM + LAT_DIM:].astype(BF16)
    wq_pad, wq_rot, wk_pad, wv = _mla_weights(w_uq, w_ukv)
    place, placer = _rope_placement()
    lane_head = jnp.arange(RW_DIM) // RW_HEAD
    hsum = (lane_head[:, None] == lane_head[None, :]).astype(BF16)
    row1 = lambda a, l: a[l].reshape(1, -1)
    tm_r = min(TM_ROUTE, S)
    ltri = (jnp.arange(tm_r)[None, :] < jnp.arange(tm_r)[:, None]).astype(BF16)
    rw_pad = jnp.pad(router_w, ((0, 0), (0, 0), (0, LANES - N_EXPERTS)))
    rb_pad = jnp.pad(router_b, ((0, 0), (0, LANES - N_EXPERTS)), constant_values=NEG_BIG)

    u = _lnmod(x, mod[0])
    for l in range(L):
        u2 = u.reshape(n, D_MODEL)
        ps = _mm(u2, w_shift[l], F32).reshape(B, S, SHIFT_DIM)
        lat = _mm(u2, w_lat[l], F32).reshape(B, S, LAT_PAD)
        sg = _mm(u2, w_gates[l], BF16, act="sigmoid").reshape(B, S, 2 * D_MODEL)
        ya = _rwkv(ps, row1(shift_mu, l), row1(decay_w0, l), decay_up[l].astype(BF16), row1(aaa_a0, l),
                   aaa_up[l].astype(BF16), gate_up[l].astype(BF16), row1(k_k, l), row1(k_a, l),
                   row1(r_k, l), row1(gn_w, l), row1(gn_b, l), hsum)
        q, k, v = _mla_prep(lat, cos_t, sin_t, row1(q_norm_w, l), wq_pad[l], wq_rot[l], row1(kv_norm_w, l),
                            wk_pad[l], wv[l], place, placer)
        yb = _attn(q, k, v)
        moe_layer = l % 2 == 1
        x1, h = _merge_out(x, ya, yb, sg, p_a[l].astype(BF16), p_b[l].astype(BF16), w_o[l].astype(BF16),
                           mod[l], row1(ln1_w, l), row1(ln1_b, l), F32 if moe_layer else BF16)
        modn = mod[min(l + 1, L - 1)]
        if moe_layer:
            x, u = _moe(x1, h, rw_pad[l // 2], rb_pad[l // 2].reshape(1, LANES), ltri,
                        moe_w_gate[l // 2].astype(BF16), moe_w_up[l // 2].astype(BF16),
                        moe_w_down[l // 2].astype(BF16), mod[l], modn, row1(ln2_w, l), row1(ln2_b, l))
        else:
            x, u = _ffn(x1, h, ffn_w_gate[l // 2].astype(BF16), ffn_w_up[l // 2].astype(BF16),
                        ffn_w_down[l // 2].astype(BF16), mod[l], modn, row1(ln2_w, l), row1(ln2_b, l))
    return x
```

```python
import functools
import math

import jax
import jax.numpy as jnp
from jax import lax
from jax.experimental import pallas as pl
from jax.experimental.pallas import tpu as pltpu

F32 = jnp.float32
BF16 = jnp.bfloat16

D_MODEL = 1024
DEPTH = 4
RW_HEADS = 8
RW_HEAD = 64
RW_DIM = RW_HEADS * RW_HEAD
DECAY_LORA = 64
AAA_LORA = 64
GATE_LORA = 128
GN_EPS = 64e-5
MLA_HEADS = 8
QK_NOPE = 64
QK_ROPE = 32
V_HEAD = 64
Q_LORA = 384
KV_LORA = 256
MLA_DIM = MLA_HEADS * V_HEAD
ROPE_THETA = 10000.0
SHIFT_DIM = 3 * RW_DIM + DECAY_LORA + AAA_LORA + GATE_LORA
LAT_DIM = Q_LORA + KV_LORA + QK_ROPE
LAT_PAD = 768
D_FF = 2816
N_EXPERTS = 8
TOP_K = 2
D_FF_EXPERT = 3584
ALPHA = (2.0 * DEPTH) ** 0.25
LN_EPS = 1e-5
RMS_EPS = 1e-6
LANES = 128
SUBLANES = 8
HEAD_PAD = 128
NEG_BIG = -1e30

RW_CHUNK = 64
RW_SEQS = 4
TM_MM = 512
TM_PREP = 256
TQ = 1024
TM_MERGE = 512
TM_FFN = 512
FF_CHUNK = 256
TM_ROUTE = 512
TM_EXP = 512
EXP_NJ = 2
TM_COMB = 256

_HI = lax.Precision.HIGHEST


def _cparams(*sem):
    return pltpu.CompilerParams(dimension_semantics=sem)


def _dot(a, b):
    return jnp.dot(a.astype(BF16), b.astype(BF16), preferred_element_type=F32)


def _dot_nt(a, b):
    return lax.dot_general(a.astype(BF16), b.astype(BF16), (((1,), (1,)), ((), ())),
                           preferred_element_type=F32)


def _dot_tn(a, b):
    return lax.dot_general(a.astype(BF16), b.astype(BF16), (((0,), (0,)), ((), ())),
                           preferred_element_type=F32)


def _split(x, n):
    parts = []
    for _ in range(n - 1):
        p = x.astype(BF16)
        parts.append(p)
        x = x - p.astype(F32)
    parts.append(x.astype(BF16))
    return parts


def _dot_split_lhs(x, w_bf16, n):
    acc = None
    for p in _split(x, n):
        t = jnp.dot(p, w_bf16, preferred_element_type=F32)
        acc = t if acc is None else acc + t
    return acc


def _dot_split_rhs(w_bf16, x, n):
    acc = None
    for p in _split(x, n):
        t = jnp.dot(w_bf16, p, preferred_element_type=F32)
        acc = t if acc is None else acc + t
    return acc


def _layer_norm(x):
    mu = jnp.mean(x, -1, keepdims=True)
    xc = x - mu
    var = jnp.mean(xc * xc, -1, keepdims=True)
    return xc * lax.rsqrt(var + LN_EPS)


def _full(shape):
    nd = len(shape)
    return pl.BlockSpec(shape, lambda *_: (0,) * nd)


def _ada_kernel(c_ref, w_ref, b_ref, o_ref):
    c = c_ref[...]
    ca = c * jax.nn.sigmoid(c)
    o_ref[0] = jnp.dot(ca, w_ref[0], preferred_element_type=F32, precision=_HI) + b_ref[0]


def _ada(c, w_ada, b_ada):
    L, _, six_d = w_ada.shape
    B = c.shape[0]
    nj = six_d // D_MODEL
    return pl.pallas_call(
        _ada_kernel,
        grid=(L, nj),
        in_specs=[
            pl.BlockSpec((B, D_MODEL), lambda l, j: (0, 0)),
            pl.BlockSpec((1, D_MODEL, D_MODEL), lambda l, j: (l, 0, j)),
            pl.BlockSpec((1, 1, D_MODEL), lambda l, j: (l, 0, j)),
        ],
        out_specs=pl.BlockSpec((1, B, D_MODEL), lambda l, j: (l, 0, j)),
        out_shape=jax.ShapeDtypeStruct((L, B, six_d), F32),
        compiler_params=_cparams("parallel", "parallel"),
        name="ada",
    )(c, w_ada, b_ada.reshape(L, 1, six_d))


def _lnmod_kernel(x_ref, mod_ref, o_ref):
    m = mod_ref[0]
    o_ref[0] = (_layer_norm(x_ref[0]) * (1.0 + m[1:2]) + m[0:1]).astype(o_ref.dtype)


def _lnmod(x, mod):
    B, S, _ = x.shape
    tm = min(TM_MM, S)
    return pl.pallas_call(
        _lnmod_kernel,
        grid=(B, S // tm),
        in_specs=[
            pl.BlockSpec((1, tm, D_MODEL), lambda b, i: (b, i, 0)),
            pl.BlockSpec((1, 6, D_MODEL), lambda b, i: (b, 0, 0)),
        ],
        out_specs=pl.BlockSpec((1, tm, D_MODEL), lambda b, i: (b, i, 0)),
        out_shape=jax.ShapeDtypeStruct((B, S, D_MODEL), BF16),
        compiler_params=_cparams("parallel", "parallel"),
        name="lnmod",
    )(x, mod)


def _proj_kernel(u_ref, ws_ref, wl_ref, wg_ref, ps_ref, lat_ref, sg_ref):
    u = u_ref[...]
    ps_ref[...] = jnp.dot(u, ws_ref[...], preferred_element_type=F32)
    lat_ref[...] = jnp.dot(u, wl_ref[...], preferred_element_type=F32)
    sg_ref[...] = jax.nn.sigmoid(jnp.dot(u, wg_ref[...], preferred_element_type=F32)).astype(sg_ref.dtype)


def _proj(u2, w_shift, w_lat, w_gates):
    n = u2.shape[0]
    tm = min(TM_MM, n)
    res = lambda w: pl.BlockSpec(w.shape, lambda i: (0, 0), pipeline_mode=pl.Buffered(1))
    rows = lambda c: pl.BlockSpec((tm, c), lambda i: (i, 0))
    return pl.pallas_call(
        _proj_kernel,
        grid=(n // tm,),
        in_specs=[rows(D_MODEL), res(w_shift), res(w_lat), res(w_gates)],
        out_specs=[rows(SHIFT_DIM), rows(LAT_PAD), rows(2 * D_MODEL)],
        out_shape=[
            jax.ShapeDtypeStruct((n, SHIFT_DIM), F32),
            jax.ShapeDtypeStruct((n, LAT_PAD), F32),
            jax.ShapeDtypeStruct((n, 2 * D_MODEL), BF16),
        ],
        compiler_params=_cparams("parallel"),
        name="proj",
    )(u2, w_shift, w_lat, w_gates)


def _rwkv_kernel(ps_ref, mu_ref, w0_ref, dup_ref, a0_ref, aup_ref, gup_ref, kk_ref, ka_ref, rk_ref,
                 gnw_ref, gnb_ref, hsum_ref, o_ref, state_ref, prev_ref, y_ref):
    C = RW_CHUNK
    N = RW_HEAD
    G = ps_ref.shape[0]

    @pl.when(pl.program_id(1) == 0)
    def _init():
        state_ref[...] = jnp.zeros_like(state_ref)
        prev_ref[...] = jnp.zeros_like(prev_ref)

    half = RW_DIM // 2
    hs_lo = hsum_ref[0:half, 0:half]
    hs_hi = hsum_ref[half:, half:]

    def head_sum(z):
        zb = z.astype(BF16)
        return jnp.concatenate([jnp.dot(zb[:, 0:half], hs_lo, preferred_element_type=F32),
                                jnp.dot(zb[:, half:], hs_hi, preferred_element_type=F32)], axis=1)

    ri = lax.broadcasted_iota(jnp.int32, (C, C), 0)
    ci = lax.broadcasted_iota(jnp.int32, (C, C), 1)
    tri = (ci <= ri).astype(BF16)
    r4 = lax.broadcasted_iota(jnp.int32, (2 * C, 4 * C), 0)
    c4 = lax.broadcasted_iota(jnp.int32, (2 * C, 4 * C), 1)
    keep = (c4 % C) < (r4 % C) + (r4 >= C).astype(jnp.int32)
    r1 = lax.broadcasted_iota(jnp.int32, (C, 2 * C), 0)
    c1 = lax.broadcasted_iota(jnp.int32, (C, 2 * C), 1)
    low = c1 < C
    eye_hi = (c1 == r1 + C).astype(F32)

    prep = []
    for s in range(G):
        p = ps_ref[s]
        row = lax.broadcasted_iota(jnp.int32, p.shape, 0)
        shifted = jnp.where(row == 0, prev_ref[s, 0:1, :], pltpu.roll(p, 1, 0))
        prev_ref[s, 0:1, :] = p[C - 1:C, :]
        xs = p + (shifted - p) * mu_ref[...]
        r = xs[:, 0:RW_DIM]
        k = xs[:, RW_DIM:2 * RW_DIM]
        v = xs[:, 2 * RW_DIM:3 * RW_DIM]
        o = 3 * RW_DIM
        dw = xs[:, o:o + DECAY_LORA]
        da = xs[:, o + DECAY_LORA:o + DECAY_LORA + AAA_LORA]
        dg = xs[:, o + DECAY_LORA + AAA_LORA:SHIFT_DIM]
        lw = -math.exp(-0.5) * jax.nn.sigmoid(w0_ref[...] + _dot(jnp.tanh(dw), dup_ref[...]))
        a = jax.nn.sigmoid(a0_ref[...] + _dot(da, aup_ref[...]))
        gate = _dot(jax.nn.sigmoid(dg), gup_ref[...])
        kk = k * kk_ref[...]
        kk = kk * lax.rsqrt(jnp.maximum(head_sum(kk * kk), 1e-24))
        k_mod = k * (1.0 + (a - 1.0) * ka_ref[...])
        b = kk * a
        bonus = head_sum(r * k_mod * rk_ref[...]) * v
        cw = _dot_split_rhs(tri, lw, 2)
        cw_last = cw[C - 1:C, :]
        e_neg = jnp.exp(-cw)
        e_rem = jnp.exp(cw_last - cw)
        prep.append(dict(
            kp=(kk * jnp.exp(cw - lw)).astype(BF16), rp=(r * jnp.exp(cw)).astype(BF16),
            bm=(b * e_neg).astype(BF16), km=(k_mod * e_neg).astype(BF16),
            bt=(b * e_rem).astype(BF16), kt=(k_mod * e_rem).astype(BF16),
            v=v.astype(BF16), w_c=jnp.exp(cw_last), bonus=bonus, gate=gate))

    chains = [(s, h) for s in range(G) for h in range(RW_HEADS)]
    sl = lambda h: slice(h * N, (h + 1) * N)

    krs, aas = [], []
    for s, h in chains:
        d = prep[s]
        kr = jnp.concatenate([d["kp"][:, sl(h)], d["rp"][:, sl(h)]], axis=0)
        bk = jnp.concatenate([d["bm"][:, sl(h)], d["km"][:, sl(h)], d["km"][:, sl(h)], d["bm"][:, sl(h)]],
                             axis=0)
        krs.append(kr)
        aas.append(jnp.where(keep, _dot_nt(kr, bk), 0.0))
    s0s = [state_ref[s, h] for s, h in chains]
    xxs = [_dot_nt(kr, s0) for kr, s0 in zip(krs, s0s)]
    rts = [jnp.where(low, _dot(aa[0:C, 0:C], aa[0:C, 0:2 * C]), eye_hi - aa[0:C, 2 * C:4 * C]) for aa in aas]
    for _ in range(int(math.log2(C)) - 1):
        outs = [_dot(rt[:, 0:C], rt) for rt in rts]
        rts = [jnp.where(low, out, rt + out) for rt, out in zip(rts, outs)]
    vs = [prep[s]["v"][:, sl(h)] for s, h in chains]
    gms = [xx[0:C] + _dot(aa[0:C, 2 * C:3 * C], v_h) for xx, aa, v_h in zip(xxs, aas, vs)]
    us = [-_dot(rt[:, C:2 * C], gm) for rt, gm in zip(rts, gms)]
    uvs = [jnp.concatenate([u.astype(BF16), v_h], axis=0) for u, v_h in zip(us, vs)]
    ys = [xx[C:2 * C] + _dot(aa[C:2 * C, 0:2 * C], uv) for xx, aa, uv in zip(xxs, aas, uvs)]
    news = []
    for (s, h), s0, uv in zip(chains, s0s, uvs):
        d = prep[s]
        btkt = jnp.concatenate([d["bt"][:, sl(h)], d["kt"][:, sl(h)]], axis=0)
        news.append(s0 * d["w_c"][:, sl(h)] + _dot_tn(uv, btkt))
    for (s, h), y_h, new in zip(chains, ys, news):
        y_ref[s, :, sl(h)] = y_h
        state_ref[s, h] = new

    for s in range(G):
        d = prep[s]
        y = y_ref[s]
        mean = head_sum(y) * (1.0 / N)
        yc = y - mean
        var = head_sum(yc * yc) * (1.0 / N)
        yn = yc * lax.rsqrt(var + GN_EPS) * gnw_ref[...] + gnb_ref[...]
        o_ref[s] = ((yn + d["bonus"]) * d["gate"]).astype(o_ref.dtype)


def _rwkv(ps, mu, w0, dup, a0, aup, gup, k_k, k_a, r_k, gn_w, gn_b, hsum):
    B, S, _ = ps.shape
    C = RW_CHUNK
    G = RW_SEQS if B % RW_SEQS == 0 else 1
    row = lambda n: pl.BlockSpec((1, n), lambda b, j: (0, 0))
    return pl.pallas_call(
        _rwkv_kernel,
        grid=(B // G, S // C),
        in_specs=[
            pl.BlockSpec((G, C, SHIFT_DIM), lambda b, j: (b, j, 0)),
            row(SHIFT_DIM), row(RW_DIM),
            pl.BlockSpec((DECAY_LORA, RW_DIM), lambda b, j: (0, 0)),
            row(RW_DIM),
            pl.BlockSpec((AAA_LORA, RW_DIM), lambda b, j: (0, 0)),
            pl.BlockSpec((GATE_LORA, RW_DIM), lambda b, j: (0, 0)),
            row(RW_DIM), row(RW_DIM), row(RW_DIM), row(RW_DIM), row(RW_DIM),
            pl.BlockSpec((RW_DIM, RW_DIM), lambda b, j: (0, 0)),
        ],
        out_specs=pl.BlockSpec((G, C, RW_DIM), lambda b, j: (b, j, 0)),
        out_shape=jax.ShapeDtypeStruct((B, S, RW_DIM), BF16),
        scratch_shapes=[
            pltpu.VMEM((G, RW_HEADS, RW_HEAD, RW_HEAD), F32),
            pltpu.VMEM((G, 8, SHIFT_DIM), F32),
            pltpu.VMEM((G, C, RW_DIM), F32),
        ],
        compiler_params=_cparams("parallel", "arbitrary"),
        name="rwkv",
    )(ps, mu, w0, dup, a0, aup, gup, k_k, k_a, r_k, gn_w, gn_b, hsum)


def _rope_kernel(pos_ref, cos_ref, sin_ref):
    half = QK_ROPE // 2
    lane = lax.broadcasted_iota(jnp.int32, (1, HEAD_PAD), 1)
    in_rope = (lane >= QK_NOPE) & (lane < QK_NOPE + QK_ROPE)
    fidx = jnp.where(lane < QK_NOPE + half, lane - QK_NOPE, lane - QK_NOPE - half)
    inv_freq = jnp.exp(-math.log(ROPE_THETA) * fidx.astype(F32) / half)
    ang = pos_ref[0] * inv_freq
    cos_ref[0] = jnp.where(in_rope, jnp.cos(ang), 0.0)
    sin_ref[0] = jnp.where(in_rope, jnp.sin(ang), 0.0)


def _rope_tables(posf):
    B, S, _ = posf.shape
    tm = min(TM_PREP, S)
    out = pl.BlockSpec((1, tm, HEAD_PAD), lambda b, i: (b, i, 0))
    return pl.pallas_call(
        _rope_kernel,
        grid=(B, S // tm),
        in_specs=[pl.BlockSpec((1, tm, 1), lambda b, i: (b, i, 0))],
        out_specs=[out, out],
        out_shape=[jax.ShapeDtypeStruct((B, S, HEAD_PAD), F32)] * 2,
        compiler_params=_cparams("parallel", "parallel"),
        name="rope_tables",
    )(posf)


def _mla_prep_kernel(lat_ref, cos_ref, sin_ref, qnw_ref, wq_ref, wqr_ref, kvnw_ref, wk_ref, wv_ref,
                     place_ref, placer_ref, q_ref, k_ref, v_ref):
    lat = lat_ref[0]
    c_q = lat[:, 0:Q_LORA]
    c_kv = lat[:, Q_LORA:Q_LORA + KV_LORA]
    k_rope = lat[:, Q_LORA + KV_LORA:LAT_DIM]

    def rms(z, w):
        return z * lax.rsqrt(jnp.mean(z * z, -1, keepdims=True) + RMS_EPS) * w

    lane = lax.broadcasted_iota(jnp.int32, (1, HEAD_PAD), 1)
    cos_r = cos_ref[0]
    sin_r = sin_ref[0]
    scale = (QK_NOPE + QK_ROPE) ** -0.5
    q_mul = jnp.where(lane < QK_NOPE, 1.0, cos_r) * scale
    q_mul_rot = sin_r * scale

    cq_n = rms(c_q, qnw_ref[...]).astype(BF16)
    q_all = jnp.dot(cq_n, wq_ref[...], preferred_element_type=F32)
    q_rot = jnp.dot(cq_n, wqr_ref[...], preferred_element_type=F32)
    ckv_n = rms(c_kv, kvnw_ref[...]).astype(BF16)
    k_all = jnp.dot(ckv_n, wk_ref[...], preferred_element_type=F32)
    v_all_t = lax.dot_general(wv_ref[...], ckv_n, (((1,), (1,)), ((), ())), preferred_element_type=F32)
    kr_pad = (_dot_split_lhs(k_rope, place_ref[...], 3) * cos_r
              + _dot_split_lhs(k_rope, placer_ref[...], 3) * sin_r)
    for h in range(MLA_HEADS):
        sl = slice(h * HEAD_PAD, (h + 1) * HEAD_PAD)
        q_ref[0, h] = (q_all[:, sl] * q_mul + q_rot[:, sl] * q_mul_rot).astype(q_ref.dtype)
        k_ref[0, h] = (k_all[:, sl] + kr_pad).astype(k_ref.dtype)
    for hp in range(MLA_HEADS // 2):
        v_ref[0, hp] = v_all_t[hp * LANES:(hp + 1) * LANES, :].astype(v_ref.dtype)


def _mla_prep(lat, cos_t, sin_t, qnw, wq, wqr, kvnw, wk, wv, place, placer):
    B, S, _ = lat.shape
    tm = min(TM_PREP, S)
    H = MLA_HEADS
    cst = lambda shape: pl.BlockSpec(shape, lambda b, i: (0,) * len(shape))
    return pl.pallas_call(
        _mla_prep_kernel,
        grid=(B, S // tm),
        in_specs=[
            pl.BlockSpec((1, tm, LAT_PAD), lambda b, i: (b, i, 0)),
            pl.BlockSpec((1, tm, HEAD_PAD), lambda b, i: (b, i, 0)),
            pl.BlockSpec((1, tm, HEAD_PAD), lambda b, i: (b, i, 0)),
            cst((1, Q_LORA)), cst((Q_LORA, H * HEAD_PAD)), cst((Q_LORA, H * HEAD_PAD)),
            cst((1, KV_LORA)), cst((KV_LORA, H * HEAD_PAD)), cst((MLA_DIM, KV_LORA)),
            cst((QK_ROPE, HEAD_PAD)), cst((QK_ROPE, HEAD_PAD)),
        ],
        out_specs=[
            pl.BlockSpec((1, H, tm, HEAD_PAD), lambda b, i: (b, 0, i, 0)),
            pl.BlockSpec((1, H, tm, HEAD_PAD), lambda b, i: (b, 0, i, 0)),
            pl.BlockSpec((1, H // 2, LANES, tm), lambda b, i: (b, 0, 0, i)),
        ],
        out_shape=[
            jax.ShapeDtypeStruct((B, H, S, HEAD_PAD), BF16),
            jax.ShapeDtypeStruct((B, H, S, HEAD_PAD), BF16),
            jax.ShapeDtypeStruct((B, H // 2, LANES, S), BF16),
        ],
        compiler_params=_cparams("parallel", "parallel"),
        name="mla_prep",
    )(lat, cos_t, sin_t, qnw, wq, wqr, kvnw, wk, wv, place, placer)


def _attn_kernel(q_ref, k_ref, vt_ref, o_ref, *, tq):
    i = pl.program_id(2)
    ki = lax.broadcasted_iota(jnp.int32, (tq, tq), 0)
    qi = lax.broadcasted_iota(jnp.int32, (tq, tq), 1)
    row_v = lax.broadcasted_iota(jnp.int32, (LANES, tq), 0)
    one = jnp.ones((), BF16)
    qs = [q_ref[0, hh] for hh in range(2)]

    def step(j, carry, masked):
        start = pl.multiple_of(j * tq, tq)
        vt = vt_ref[0, 0, :, pl.ds(start, tq)]
        v_aug = [jnp.where(row_v < V_HEAD, vt, one), jnp.where(row_v < V_HEAD, one, vt)]
        ss = [lax.dot_general(k_ref[0, hh, pl.ds(start, tq), :], qs[hh], (((1,), (1,)), ((), ())),
                              preferred_element_type=F32) for hh in range(2)]
        new = []
        ps = []
        for hh in range(2):
            m, acc = carry[hh]
            s = ss[hh]
            if masked:
                s = jnp.where(ki <= qi, s, NEG_BIG)
            m_new = jnp.maximum(m, jnp.max(s, 0, keepdims=True))
            ps.append(jnp.exp(s - m_new).astype(BF16))
            new.append((m_new, jnp.exp(m - m_new) * acc))
        pvs = [jnp.dot(v_aug[hh], ps[hh], preferred_element_type=F32) for hh in range(2)]
        return tuple((new[hh][0], new[hh][1] + pvs[hh]) for hh in range(2))

    init1 = (jnp.full((1, tq), NEG_BIG, F32), jnp.zeros((LANES, tq), F32))
    carry = lax.fori_loop(0, i, functools.partial(step, masked=False), (init1, init1))
    carry = step(i, carry, True)
    a0 = carry[0][1]
    a1 = carry[1][1]
    out_t = jnp.concatenate([a0[0:V_HEAD] / a0[V_HEAD:], a1[V_HEAD:] / a1[0:V_HEAD]], axis=0)
    o_ref[0] = out_t.T.astype(o_ref.dtype)


def _attn(q, k, vt):
    B, H, S, _ = q.shape
    tq = min(TQ, S)
    return pl.pallas_call(
        functools.partial(_attn_kernel, tq=tq),
        grid=(B, H // 2, S // tq),
        in_specs=[
            pl.BlockSpec((1, 2, tq, HEAD_PAD), lambda b, p, i: (b, p, i, 0)),
            pl.BlockSpec((1, 2, S, HEAD_PAD), lambda b, p, i: (b, p, 0, 0)),
            pl.BlockSpec((1, 1, LANES, S), lambda b, p, i: (b, p, 0, 0)),
        ],
        out_specs=pl.BlockSpec((1, tq, LANES), lambda b, p, i: (b, i, p)),
        out_shape=jax.ShapeDtypeStruct((B, S, MLA_DIM), BF16),
        compiler_params=_cparams("parallel", "parallel", "arbitrary"),
        name="attn",
    )(q, k, vt)


def _residual_epilogue(x, f, gate, ln_w, ln_b):
    return _layer_norm(ALPHA * x + gate * f) * ln_w + ln_b


def _merge_out_kernel(x_ref, ya_ref, yb_ref, sga_ref, sgb_ref, pa_ref, pb_ref, wo_ref, mod_ref,
                      lnw_ref, lnb_ref, x1_ref, h_ref):
    m = mod_ref[0]
    tm = x_ref.shape[1]
    halves = [slice(0, tm // 2), slice(tm // 2, tm)]
    mas = [jnp.dot(ya_ref[0, r, :], pa_ref[...], preferred_element_type=F32) for r in halves]
    mbs = [jnp.dot(yb_ref[0, r, :], pb_ref[...], preferred_element_type=F32) for r in halves]
    merged = [(sga_ref[0, r, :].astype(F32) * ma + sgb_ref[0, r, :].astype(F32) * mb).astype(BF16)
              for r, ma, mb in zip(halves, mas, mbs)]
    mixes = [jnp.dot(mg, wo_ref[...], preferred_element_type=F32) for mg in merged]
    for r, mix in zip(halves, mixes):
        x1 = _residual_epilogue(x_ref[0, r, :], mix, m[2:3], lnw_ref[...], lnb_ref[...])
        x1_ref[0, r, :] = x1
        h_ref[0, r, :] = (_layer_norm(x1) * (1.0 + m[4:5]) + m[3:4]).astype(h_ref.dtype)


def _merge_out(x, ya, yb, sg, pa, pb, wo, mod, ln_w, ln_b, h_dtype):
    B, S, _ = x.shape
    tm = min(TM_MERGE, S)
    cst = lambda shape: pl.BlockSpec(shape, lambda b, i: (0,) * len(shape))
    rows = lambda n: pl.BlockSpec((1, tm, n), lambda b, i: (b, i, 0))
    return pl.pallas_call(
        _merge_out_kernel,
        grid=(B, S // tm),
        in_specs=[
            rows(D_MODEL), rows(RW_DIM), rows(MLA_DIM),
            pl.BlockSpec((1, tm, D_MODEL), lambda b, i: (b, i, 0)),
            pl.BlockSpec((1, tm, D_MODEL), lambda b, i: (b, i, 1)),
            cst((RW_DIM, D_MODEL)), cst((MLA_DIM, D_MODEL)), cst((D_MODEL, D_MODEL)),
            pl.BlockSpec((1, 6, D_MODEL), lambda b, i: (b, 0, 0)),
            cst((1, D_MODEL)), cst((1, D_MODEL)),
        ],
        out_specs=[rows(D_MODEL), rows(D_MODEL)],
        out_shape=[
            jax.ShapeDtypeStruct((B, S, D_MODEL), F32),
            jax.ShapeDtypeStruct((B, S, D_MODEL), h_dtype),
        ],
        compiler_params=_cparams("parallel", "parallel"),
        name="merge_out",
    )(x, ya, yb, sg, sg, pa, pb, wo, mod, ln_w, ln_b)


def _write_epilogue(x1, f, m, modn_ref, lnw_ref, lnb_ref, x2_ref, u_ref):
    x2 = _residual_epilogue(x1, f, m[5:6], lnw_ref[...], lnb_ref[...])
    x2_ref[0] = x2
    mn = modn_ref[0]
    u_ref[0] = (_layer_norm(x2) * (1.0 + mn[1:2]) + mn[0:1]).astype(u_ref.dtype)


def _ffn_kernel(x1_ref, h_ref, wg_ref, wu_ref, wd_ref, mod_ref, modn_ref, lnw_ref, lnb_ref,
                x2_ref, u_ref):
    h = h_ref[0]
    acc = None
    for j in range(D_FF // FF_CHUNK):
        sl = slice(j * FF_CHUNK, (j + 1) * FF_CHUNK)
        gt = jnp.dot(h, wg_ref[:, sl], preferred_element_type=F32)
        up = jnp.dot(h, wu_ref[:, sl], preferred_element_type=F32)
        act = (gt * jax.nn.sigmoid(gt) * up).astype(BF16)
        t = jnp.dot(act, wd_ref[sl, :], preferred_element_type=F32)
        acc = t if acc is None else acc + t
    _write_epilogue(x1_ref[0], acc, mod_ref[0], modn_ref, lnw_ref, lnb_ref, x2_ref, u_ref)


def _ffn(x1, h, wg, wu, wd, mod, modn, ln_w, ln_b):
    B, S, _ = x1.shape
    tm = min(TM_FFN, S)
    res = lambda shape: pl.BlockSpec(shape, lambda b, i: (0,) * len(shape), pipeline_mode=pl.Buffered(1))
    cst = lambda shape: pl.BlockSpec(shape, lambda b, i: (0,) * len(shape))
    rows = pl.BlockSpec((1, tm, D_MODEL), lambda b, i: (b, i, 0))
    modspec = pl.BlockSpec((1, 6, D_MODEL), lambda b, i: (b, 0, 0))
    return pl.pallas_call(
        _ffn_kernel,
        grid=(B, S // tm),
        in_specs=[rows, rows, res((D_MODEL, D_FF)), res((D_MODEL, D_FF)), res((D_FF, D_MODEL)),
                  modspec, modspec, cst((1, D_MODEL)), cst((1, D_MODEL))],
        out_specs=[rows, rows],
        out_shape=[
            jax.ShapeDtypeStruct((B, S, D_MODEL), F32),
            jax.ShapeDtypeStruct((B, S, D_MODEL), BF16),
        ],
        compiler_params=_cparams("parallel", "parallel"),
        name="ffn",
    )(x1, h, wg, wu, wd, mod, modn, ln_w, ln_b)


def _router_kernel(h_ref, rw_ref, rb_ref, ltri_ref, rec_ref, cnt_ref, carry_ref):
    first = (pl.program_id(0) == 0) & (pl.program_id(1) == 0)

    @pl.when(first)
    def _init():
        carry_ref[...] = jnp.zeros_like(carry_ref)

    h = h_ref[0]
    tm = h.shape[0]
    logits = jnp.dot(h, rw_ref[...], preferred_element_type=F32, precision=_HI) + rb_ref[...]
    lane = lax.broadcasted_iota(jnp.int32, (tm, LANES), 1)
    m1 = jnp.max(logits, -1, keepdims=True)
    i1 = jnp.min(jnp.where(logits == m1, lane, LANES), -1, keepdims=True)
    rest = jnp.where(lane == i1, -jnp.inf, logits)
    m2 = jnp.max(rest, -1, keepdims=True)
    i2 = jnp.min(jnp.where(rest == m2, lane, LANES), -1, keepdims=True)
    e21 = jnp.exp(m2 - m1)
    g1 = 1.0 / (1.0 + e21)
    g2 = e21 / (1.0 + e21)
    oh1 = lane == i1
    oh2 = lane == i2
    cnt = oh1.astype(F32) + oh2.astype(F32)
    before = jnp.dot(ltri_ref[...], cnt.astype(BF16), preferred_element_type=F32) + carry_ref[0:1, :]
    rank1 = jnp.sum(jnp.where(oh1, before, 0.0), -1, keepdims=True)
    rank2 = jnp.sum(jnp.where(oh2, before, 0.0), -1, keepdims=True)
    carry_ref[0:1, :] = carry_ref[0:1, :] + jnp.sum(cnt, 0, keepdims=True)
    cols = (i1.astype(F32), i2.astype(F32), g1, g2, rank1, rank2)
    rec = jnp.zeros((tm, LANES), F32)
    for c, val in enumerate(cols):
        rec = jnp.where(lane == c, val, rec)
    rec_ref[0] = rec
    cnt_ref[...] = carry_ref[...]


def _router(h, rw_pad, rb_pad, ltri):
    B, S, _ = h.shape
    tm = min(TM_ROUTE, S)
    return pl.pallas_call(
        _router_kernel,
        grid=(B, S // tm),
        in_specs=[
            pl.BlockSpec((1, tm, D_MODEL), lambda b, i: (b, i, 0)),
            pl.BlockSpec((D_MODEL, LANES), lambda b, i: (0, 0)),
            pl.BlockSpec((1, LANES), lambda b, i: (0, 0)),
            pl.BlockSpec((tm, tm), lambda b, i: (0, 0)),
        ],
        out_specs=[
            pl.BlockSpec((1, tm, LANES), lambda b, i: (b, i, 0)),
            pl.BlockSpec((8, LANES), lambda b, i: (0, 0)),
        ],
        out_shape=[
            jax.ShapeDtypeStruct((B, S, LANES), F32),
            jax.ShapeDtypeStruct((8, LANES), F32),
        ],
        scratch_shapes=[pltpu.VMEM((8, LANES), F32)],
        compiler_params=_cparams("arbitrary", "arbitrary"),
        name="router",
    )(h, rw_pad, rb_pad, ltri)


def _expert_kernel(be_ref, nu_ref, tok_ref, h_hbm, wg_ref, wu_ref, wd_ref, ys_ref,
                   rows_ref, x_ref, sems, *, tm):
    i = pl.program_id(0)
    j = pl.program_id(1)
    nb = pl.num_programs(0)
    nj = pl.num_programs(1)
    per_step = tm // EXP_NJ
    used = i < nu_ref[0]

    def row(slot_index, slot, part, r):
        return pltpu.make_async_copy(h_hbm.at[pl.ds(tok_ref[slot_index], 1)],
                                     rows_ref.at[slot, part, pl.ds(r, 1)], sems.at[slot])

    def wait_block(slot):
        def wait(r, _):
            row(0, slot, 0, 0).wait()
            return 0
        lax.fori_loop(0, tm, wait, 0, unroll=8)

    @pl.when((i == 0) & (j == 0))
    def _first():
        for part in range(EXP_NJ):
            def start(r, _):
                row(part * per_step + r, 0, part, r).start()
                return 0
            lax.fori_loop(0, per_step, start, 0, unroll=8)

    @pl.when(j == 0)
    def _arrive():
        wait_block(i % 2)
        for part in range(EXP_NJ):
            x_ref[part * per_step:(part + 1) * per_step, :] = rows_ref[i % 2, part].astype(BF16)

    nxt_base = jnp.minimum(i + 1, nb - 1) * tm + j * per_step
    for r in range(per_step):
        row(nxt_base + r, (i + 1) % 2, j, r).start()

    x = x_ref[...]
    acc = None
    for c in range(wg_ref.shape[2] // FF_CHUNK):
        cols = slice(c * FF_CHUNK, (c + 1) * FF_CHUNK)
        gt = jnp.dot(x, wg_ref[0, :, cols], preferred_element_type=F32)
        up = jnp.dot(x, wu_ref[0, :, cols], preferred_element_type=F32)
        act = (gt * jax.nn.sigmoid(gt) * up).astype(BF16)
        t = jnp.dot(act, wd_ref[0, cols, :], preferred_element_type=F32)
        acc = t if acc is None else acc + t
    acc = jnp.where(used, acc, 0.0)

    @pl.when(j == 0)
    def _():
        ys_ref[...] = acc

    @pl.when(j > 0)
    def _():
        ys_ref[...] += acc

    @pl.when((i == nb - 1) & (j == nj - 1))
    def _drain():
        wait_block((i + 1) % 2)


def _experts(block_e, n_used, slot_tok, h2d, wg, wu, wd):
    m_pad = slot_tok.shape[0]
    tm = TM_EXP
    nj = EXP_NJ
    tf = D_FF_EXPERT // nj

    def jj(i, j, nu):
        return jnp.where(i < nu[0], j, nj - 1)

    return pl.pallas_call(
        functools.partial(_expert_kernel, tm=tm),
        grid_spec=pltpu.PrefetchScalarGridSpec(
            num_scalar_prefetch=3,
            grid=(m_pad // tm, nj),
            in_specs=[
                pl.BlockSpec(memory_space=pl.ANY),
                pl.BlockSpec((1, D_MODEL, tf), lambda i, j, be, nu, tok: (be[i], 0, jj(i, j, nu))),
                pl.BlockSpec((1, D_MODEL, tf), lambda i, j, be, nu, tok: (be[i], 0, jj(i, j, nu))),
                pl.BlockSpec((1, tf, D_MODEL), lambda i, j, be, nu, tok: (be[i], jj(i, j, nu), 0)),
            ],
            out_specs=pl.BlockSpec((tm, D_MODEL), lambda i, j, be, nu, tok: (i, 0)),
            scratch_shapes=[pltpu.VMEM((2, nj, tm // nj, D_MODEL), F32), pltpu.VMEM((tm, D_MODEL), BF16),
                            pltpu.SemaphoreType.DMA((2,))],
        ),
        out_shape=jax.ShapeDtypeStruct((m_pad, D_MODEL), F32),
        compiler_params=_cparams("arbitrary", "arbitrary"),
        name="experts",
    )(block_e, n_used, slot_tok, h2d, wg, wu, wd)


def _combine_kernel(dest_ref, x1_ref, gate_ref, ys_hbm, mod_ref, modn_ref, lnw_ref, lnb_ref,
                    x2_ref, u_ref, buf_ref, sem, *, tm, s_len):
    base = pl.program_id(0) * s_len + pl.program_id(1) * tm

    sub = buf_ref.shape[2]

    def row(group, u, k):
        t = base + group * sub + u
        return pltpu.make_async_copy(ys_hbm.at[pl.ds(dest_ref[2 * t + k], 1)],
                                     buf_ref.at[k, group, pl.ds(u, 1)], sem)

    def start(group, _):
        for u in range(sub):
            row(group, u, 0).start()
            row(group, u, 1).start()
        return 0

    lax.fori_loop(0, tm // sub, start, 0)

    def wait(group, _):
        for u in range(sub):
            row(0, 0, 0).wait()
            row(0, 0, 1).wait()
        return 0

    lax.fori_loop(0, tm // sub, wait, 0)
    gate = gate_ref[0]
    f = gate[:, 0:1] * buf_ref[0].reshape(tm, D_MODEL) + gate[:, 1:2] * buf_ref[1].reshape(tm, D_MODEL)
    _write_epilogue(x1_ref[0], f, mod_ref[0], modn_ref, lnw_ref, lnb_ref, x2_ref, u_ref)


def _combine(dest_flat, x1, gates, ys, mod, modn, ln_w, ln_b):
    B, S, _ = x1.shape
    tm = min(TM_COMB, S)
    rows = pl.BlockSpec((1, tm, D_MODEL), lambda b, i, d: (b, i, 0))
    modspec = pl.BlockSpec((1, 6, D_MODEL), lambda b, i, d: (b, 0, 0))
    cst = pl.BlockSpec((1, D_MODEL), lambda b, i, d: (0, 0))
    return pl.pallas_call(
        functools.partial(_combine_kernel, tm=tm, s_len=S),
        grid_spec=pltpu.PrefetchScalarGridSpec(
            num_scalar_prefetch=1,
            grid=(B, S // tm),
            in_specs=[rows, pl.BlockSpec((1, tm, TOP_K), lambda b, i, d: (b, i, 0)),
                      pl.BlockSpec(memory_space=pl.ANY), modspec, modspec, cst, cst],
            out_specs=[rows, rows],
            scratch_shapes=[pltpu.VMEM((TOP_K, tm // SUBLANES, SUBLANES, D_MODEL), F32),
                            pltpu.SemaphoreType.DMA(())],
        ),
        out_shape=[
            jax.ShapeDtypeStruct((B, S, D_MODEL), F32),
            jax.ShapeDtypeStruct((B, S, D_MODEL), BF16),
        ],
        compiler_params=_cparams("arbitrary", "arbitrary"),
        name="combine",
    )(dest_flat, x1, gates, ys, mod, modn, ln_w, ln_b)


def _moe(x1, h, rw_pad, rb_pad, ltri, wg, wu, wd, mod, modn, ln_w, ln_b):
    B, S, _ = x1.shape
    n = B * S
    m = n * TOP_K
    rec, cnt = _router(h, rw_pad, rb_pad, ltri)
    rec = rec.reshape(n, LANES)
    e = rec[:, 0:TOP_K].astype(jnp.int32)
    gates = rec[:, TOP_K:2 * TOP_K]
    rank = rec[:, 2 * TOP_K:3 * TOP_K].astype(jnp.int32)
    counts = cnt[0, :N_EXPERTS].astype(jnp.int32)
    padded = (counts + TM_EXP - 1) // TM_EXP * TM_EXP
    pad_end = jnp.cumsum(padded)
    pad_start = pad_end - padded
    dest = (pad_start[e] + rank).reshape(m)
    n_blocks = m // TM_EXP + N_EXPERTS
    block_e = jnp.minimum(
        jnp.searchsorted(pad_end, jnp.arange(n_blocks, dtype=jnp.int32) * TM_EXP, side="right"),
        N_EXPERTS - 1).astype(jnp.int32)
    n_used = (pad_end[-1:] // TM_EXP).astype(jnp.int32)
    slot_tok = jnp.zeros((n_blocks * TM_EXP,), jnp.int32).at[dest].set(
        jnp.arange(m, dtype=jnp.int32) // TOP_K)
    ys = _experts(block_e, n_used, slot_tok, h.reshape(n, D_MODEL), wg, wu, wd)
    return _combine(dest, x1, gates.reshape(B, S, TOP_K), ys, mod, modn, ln_w, ln_b)


def _mla_weights(w_uq, w_ukv):
    L = w_uq.shape[0]
    H = MLA_HEADS
    half = QK_ROPE // 2
    wq = w_uq.reshape(L, Q_LORA, H, QK_NOPE + QK_ROPE)
    nope, rope = wq[..., :QK_NOPE], wq[..., QK_NOPE:]
    zpad = jnp.zeros((L, Q_LORA, H, HEAD_PAD - QK_NOPE - QK_ROPE), w_uq.dtype)
    wq_pad = jnp.concatenate([nope, rope, zpad], -1).reshape(L, Q_LORA, H * HEAD_PAD)
    rope_rot = jnp.concatenate([-rope[..., half:], rope[..., :half]], -1)
    wq_rot = jnp.concatenate([jnp.zeros_like(nope), rope_rot, zpad], -1).reshape(L, Q_LORA, H * HEAD_PAD)
    wkv = w_ukv.reshape(L, KV_LORA, H, QK_NOPE + V_HEAD)
    k_nope, v = wkv[..., :QK_NOPE], wkv[..., QK_NOPE:]
    wk_pad = jnp.concatenate(
        [k_nope, jnp.zeros((L, KV_LORA, H, HEAD_PAD - QK_NOPE), w_ukv.dtype)], -1
    ).reshape(L, KV_LORA, H * HEAD_PAD)
    wv = jnp.swapaxes(v.reshape(L, KV_LORA, MLA_DIM), 1, 2)
    return wq_pad.astype(BF16), wq_rot.astype(BF16), wk_pad.astype(BF16), wv.astype(BF16)


def _rope_placement():
    half = QK_ROPE // 2
    src = jnp.arange(QK_ROPE)[:, None]
    dst = jnp.arange(HEAD_PAD)[None, :]
    place = (dst == src + QK_NOPE).astype(BF16)
    rot = jnp.where((src < half) & (dst == src + QK_NOPE + half), 1.0,
                    jnp.where((src >= half) & (dst == src + QK_NOPE - half), -1.0, 0.0))
    return place, rot.astype(BF16)


def kernel(x, c, positions, w_ada, b_ada, w_in, shift_mu, decay_w0, decay_up, aaa_a0, aaa_up, gate_up, k_k, k_a, r_k, gn_w, gn_b, q_norm_w, w_uq, kv_norm_w, w_ukv, p_a, p_b, w_o, ln1_w, ln1_b, ln2_w, ln2_b, ffn_w_gate, ffn_w_up, ffn_w_down, router_w, router_b, moe_w_gate, moe_w_up, moe_w_down):
    B, S, _ = x.shape
    n = B * S
    L = DEPTH
    assert S % RW_CHUNK == 0 and S % LANES == 0

    mod = _ada(c, w_ada, b_ada).reshape(L, B, 6, D_MODEL)
    cos_t, sin_t = _rope_tables(positions.astype(F32).reshape(B, S, 1))

    w_shift = w_in[:, :, :SHIFT_DIM].astype(BF16)
    w_lat = jnp.pad(w_in[:, :, SHIFT_DIM:SHIFT_DIM + LAT_DIM],
                    ((0, 0), (0, 0), (0, LAT_PAD - LAT_DIM))).astype(BF16)
    w_gates = w_in[:, :, SHIFT_DIM + LAT_DIM:].astype(BF16)
    wq_pad, wq_rot, wk_pad, wv = _mla_weights(w_uq, w_ukv)
    place, placer = _rope_placement()
    lane_head = jnp.arange(RW_DIM) // RW_HEAD
    hsum = (lane_head[:, None] == lane_head[None, :]).astype(BF16)
    row1 = lambda a, l: a[l].reshape(1, -1)
    tm_r = min(TM_ROUTE, S)
    ltri = (jnp.arange(tm_r)[None, :] < jnp.arange(tm_r)[:, None]).astype(BF16)
    rw_pad = jnp.pad(router_w, ((0, 0), (0, 0), (0, LANES - N_EXPERTS)))
    rb_pad = jnp.pad(router_b, ((0, 0), (0, LANES - N_EXPERTS)), constant_values=NEG_BIG)

    u = _lnmod(x, mod[0])
    for l in range(L):
        u2 = u.reshape(n, D_MODEL)
        ps, lat, sg = _proj(u2, w_shift[l], w_lat[l], w_gates[l])
        ps = ps.reshape(B, S, SHIFT_DIM)
        lat = lat.reshape(B, S, LAT_PAD)
        sg = sg.reshape(B, S, 2 * D_MODEL)
        ya = _rwkv(ps, row1(shift_mu, l), row1(decay_w0, l), decay_up[l].astype(BF16), row1(aaa_a0, l),
                   aaa_up[l].astype(BF16), gate_up[l].astype(BF16), row1(k_k, l), row1(k_a, l),
                   row1(r_k, l), row1(gn_w, l), row1(gn_b, l), hsum)
        q, k, v = _mla_prep(lat, cos_t, sin_t, row1(q_norm_w, l), wq_pad[l], wq_rot[l], row1(kv_norm_w, l),
                            wk_pad[l], wv[l], place, placer)
        yb = _attn(q, k, v)
        moe_layer = l % 2 == 1
        x1, h = _merge_out(x, ya, yb, sg, p_a[l].astype(BF16), p_b[l].astype(BF16), w_o[l].astype(BF16),
                           mod[l], row1(ln1_w, l), row1(ln1_b, l), F32 if moe_layer else BF16)
        modn = mod[min(l + 1, L - 1)]
        if moe_layer:
            x, u = _moe(x1, h, rw_pad[l // 2], rb_pad[l // 2].reshape(1, LANES), ltri,
                        moe_w_gate[l // 2].astype(BF16), moe_w_up[l // 2].astype(BF16),
                        moe_w_down[l // 2].astype(BF16), mod[l], modn, row1(ln2_w, l), row1(ln2_b, l))
        else:
            x, u = _ffn(x1, h, ffn_w_gate[l // 2].astype(BF16), ffn_w_up[l // 2].astype(BF16),
                        ffn_w_down[l // 2].astype(BF16), mod[l], modn, row1(ln2_w, l), row1(ln2_b, l))
    return x
```

```python
import functools
import math

import jax
import jax.numpy as jnp
from jax import lax
from jax.experimental import pallas as pl
from jax.experimental.pallas import tpu as pltpu

F32 = jnp.float32
BF16 = jnp.bfloat16

D_MODEL = 1024
DEPTH = 4
RW_HEADS = 8
RW_HEAD = 64
RW_DIM = RW_HEADS * RW_HEAD
DECAY_LORA = 64
AAA_LORA = 64
GATE_LORA = 128
GN_EPS = 64e-5
MLA_HEADS = 8
QK_NOPE = 64
QK_ROPE = 32
V_HEAD = 64
Q_LORA = 384
KV_LORA = 256
MLA_DIM = MLA_HEADS * V_HEAD
ROPE_THETA = 10000.0
SHIFT_DIM = 3 * RW_DIM + DECAY_LORA + AAA_LORA + GATE_LORA
LAT_DIM = Q_LORA + KV_LORA + QK_ROPE
LAT_PAD = 768
D_FF = 2816
N_EXPERTS = 8
TOP_K = 2
D_FF_EXPERT = 3584
ALPHA = (2.0 * DEPTH) ** 0.25
LN_EPS = 1e-5
RMS_EPS = 1e-6
LANES = 128
SUBLANES = 8
HEAD_PAD = 128
NEG_BIG = -1e30

RW_CHUNK = 64
RW_SEQS = 4
TM_MM = 512
TM_PREP = 256
TQ = 1024
TM_MERGE = 512
TM_FFN = 512
FF_CHUNK = 256
TM_ROUTE = 512
TM_EXP = 512
EXP_NJ = 2
TM_COMB = 256

_HI = lax.Precision.HIGHEST


def _cparams(*sem):
    return pltpu.CompilerParams(dimension_semantics=sem)


def _dot(a, b):
    return jnp.dot(a.astype(BF16), b.astype(BF16), preferred_element_type=F32)


def _dot_nt(a, b):
    return lax.dot_general(a.astype(BF16), b.astype(BF16), (((1,), (1,)), ((), ())),
                           preferred_element_type=F32)


def _dot_tn(a, b):
    return lax.dot_general(a.astype(BF16), b.astype(BF16), (((0,), (0,)), ((), ())),
                           preferred_element_type=F32)


def _split(x, n):
    parts = []
    for _ in range(n - 1):
        p = x.astype(BF16)
        parts.append(p)
        x = x - p.astype(F32)
    parts.append(x.astype(BF16))
    return parts


def _dot_split_lhs(x, w_bf16, n):
    acc = None
    for p in _split(x, n):
        t = jnp.dot(p, w_bf16, preferred_element_type=F32)
        acc = t if acc is None else acc + t
    return acc


def _dot_split_rhs(w_bf16, x, n):
    acc = None
    for p in _split(x, n):
        t = jnp.dot(w_bf16, p, preferred_element_type=F32)
        acc = t if acc is None else acc + t
    return acc


def _layer_norm(x):
    mu = jnp.mean(x, -1, keepdims=True)
    xc = x - mu
    var = jnp.mean(xc * xc, -1, keepdims=True)
    return xc * lax.rsqrt(var + LN_EPS)


def _full(shape):
    nd = len(shape)
    return pl.BlockSpec(shape, lambda *_: (0,) * nd)


def _ada_kernel(c_ref, w_ref, b_ref, o_ref):
    c = c_ref[...]
    ca = c * jax.nn.sigmoid(c)
    o_ref[0] = jnp.dot(ca, w_ref[0], preferred_element_type=F32, precision=_HI) + b_ref[0]


def _ada(c, w_ada, b_ada):
    L, _, six_d = w_ada.shape
    B = c.shape[0]
    nj = six_d // D_MODEL
    return pl.pallas_call(
        _ada_kernel,
        grid=(L, nj),
        in_specs=[
            pl.BlockSpec((B, D_MODEL), lambda l, j: (0, 0)),
            pl.BlockSpec((1, D_MODEL, D_MODEL), lambda l, j: (l, 0, j)),
            pl.BlockSpec((1, 1, D_MODEL), lambda l, j: (l, 0, j)),
        ],
        out_specs=pl.BlockSpec((1, B, D_MODEL), lambda l, j: (l, 0, j)),
        out_shape=jax.ShapeDtypeStruct((L, B, six_d), F32),
        compiler_params=_cparams("parallel", "parallel"),
        name="ada",
    )(c, w_ada, b_ada.reshape(L, 1, six_d))


def _lnmod_kernel(x_ref, mod_ref, o_ref):
    m = mod_ref[0]
    o_ref[0] = (_layer_norm(x_ref[0]) * (1.0 + m[1:2]) + m[0:1]).astype(o_ref.dtype)


def _lnmod(x, mod):
    B, S, _ = x.shape
    tm = min(TM_MM, S)
    return pl.pallas_call(
        _lnmod_kernel,
        grid=(B, S // tm),
        in_specs=[
            pl.BlockSpec((1, tm, D_MODEL), lambda b, i: (b, i, 0)),
            pl.BlockSpec((1, 6, D_MODEL), lambda b, i: (b, 0, 0)),
        ],
        out_specs=pl.BlockSpec((1, tm, D_MODEL), lambda b, i: (b, i, 0)),
        out_shape=jax.ShapeDtypeStruct((B, S, D_MODEL), BF16),
        compiler_params=_cparams("parallel", "parallel"),
        name="lnmod",
    )(x, mod)


def _proj_kernel(u_ref, ws_ref, wl_ref, wg_ref, ps_ref, lat_ref, sg_ref):
    u = u_ref[...]
    ps_ref[...] = jnp.dot(u, ws_ref[...], preferred_element_type=F32)
    lat_ref[...] = jnp.dot(u, wl_ref[...], preferred_element_type=F32)
    sg_ref[...] = jax.nn.sigmoid(jnp.dot(u, wg_ref[...], preferred_element_type=F32)).astype(sg_ref.dtype)


def _proj(u2, w_shift, w_lat, w_gates):
    n = u2.shape[0]
    tm = min(TM_MM, n)
    res = lambda w: pl.BlockSpec(w.shape, lambda i: (0, 0), pipeline_mode=pl.Buffered(1))
    rows = lambda c: pl.BlockSpec((tm, c), lambda i: (i, 0))
    return pl.pallas_call(
        _proj_kernel,
        grid=(n // tm,),
        in_specs=[rows(D_MODEL), res(w_shift), res(w_lat), res(w_gates)],
        out_specs=[rows(SHIFT_DIM), rows(LAT_PAD), rows(2 * D_MODEL)],
        out_shape=[
            jax.ShapeDtypeStruct((n, SHIFT_DIM), F32),
            jax.ShapeDtypeStruct((n, LAT_PAD), F32),
            jax.ShapeDtypeStruct((n, 2 * D_MODEL), BF16),
        ],
        compiler_params=_cparams("parallel"),
        name="proj",
    )(u2, w_shift, w_lat, w_gates)


def _rwkv_kernel(ps_ref, mu_ref, w0_ref, dup_ref, a0_ref, aup_ref, gup_ref, kk_ref, ka_ref, rk_ref,
                 gnw_ref, gnb_ref, hsum_ref, o_ref, state_ref, prev_ref, y_ref):
    C = RW_CHUNK
    N = RW_HEAD
    G = ps_ref.shape[0]

    @pl.when(pl.program_id(1) == 0)
    def _init():
        state_ref[...] = jnp.zeros_like(state_ref)
        prev_ref[...] = jnp.zeros_like(prev_ref)

    half = RW_DIM // 2
    hs_lo = hsum_ref[0:half, 0:half]
    hs_hi = hsum_ref[half:, half:]

    def head_sum(z):
        zb = z.astype(BF16)
        return jnp.concatenate([jnp.dot(zb[:, 0:half], hs_lo, preferred_element_type=F32),
                                jnp.dot(zb[:, half:], hs_hi, preferred_element_type=F32)], axis=1)

    ri = lax.broadcasted_iota(jnp.int32, (C, C), 0)
    ci = lax.broadcasted_iota(jnp.int32, (C, C), 1)
    tri = (ci <= ri).astype(BF16)
    r4 = lax.broadcasted_iota(jnp.int32, (2 * C, 4 * C), 0)
    c4 = lax.broadcasted_iota(jnp.int32, (2 * C, 4 * C), 1)
    keep = (c4 % C) < (r4 % C) + (r4 >= C).astype(jnp.int32)
    r1 = lax.broadcasted_iota(jnp.int32, (C, 2 * C), 0)
    c1 = lax.broadcasted_iota(jnp.int32, (C, 2 * C), 1)
    low = c1 < C
    eye_hi = (c1 == r1 + C).astype(F32)

    prep = []
    for s in range(G):
        p = ps_ref[s]
        row = lax.broadcasted_iota(jnp.int32, p.shape, 0)
        shifted = jnp.where(row == 0, prev_ref[s, 0:1, :], pltpu.roll(p, 1, 0))
        prev_ref[s, 0:1, :] = p[C - 1:C, :]
        xs = p + (shifted - p) * mu_ref[...]
        r = xs[:, 0:RW_DIM]
        k = xs[:, RW_DIM:2 * RW_DIM]
        v = xs[:, 2 * RW_DIM:3 * RW_DIM]
        o = 3 * RW_DIM
        dw = xs[:, o:o + DECAY_LORA]
        da = xs[:, o + DECAY_LORA:o + DECAY_LORA + AAA_LORA]
        dg = xs[:, o + DECAY_LORA + AAA_LORA:SHIFT_DIM]
        lw = -math.exp(-0.5) * jax.nn.sigmoid(w0_ref[...] + _dot(jnp.tanh(dw), dup_ref[...]))
        a = jax.nn.sigmoid(a0_ref[...] + _dot(da, aup_ref[...]))
        gate = _dot(jax.nn.sigmoid(dg), gup_ref[...])
        kk = k * kk_ref[...]
        kk = kk * lax.rsqrt(jnp.maximum(head_sum(kk * kk), 1e-24))
        k_mod = k * (1.0 + (a - 1.0) * ka_ref[...])
        b = kk * a
        bonus = head_sum(r * k_mod * rk_ref[...]) * v
        cw = _dot_split_rhs(tri, lw, 2)
        cw_last = cw[C - 1:C, :]
        e_neg = jnp.exp(-cw)
        e_rem = jnp.exp(cw_last - cw)
        prep.append(dict(
            kp=(kk * jnp.exp(cw - lw)).astype(BF16), rp=(r * jnp.exp(cw)).astype(BF16),
            bm=(b * e_neg).astype(BF16), km=(k_mod * e_neg).astype(BF16),
            bt=(b * e_rem).astype(BF16), kt=(k_mod * e_rem).astype(BF16),
            v=v.astype(BF16), w_c=jnp.exp(cw_last), bonus=bonus, gate=gate))

    chains = [(s, h) for s in range(G) for h in range(RW_HEADS)]
    sl = lambda h: slice(h * N, (h + 1) * N)

    krs, aas = [], []
    for s, h in chains:
        d = prep[s]
        kr = jnp.concatenate([d["kp"][:, sl(h)], d["rp"][:, sl(h)]], axis=0)
        bk = jnp.concatenate([d["bm"][:, sl(h)], d["km"][:, sl(h)], d["km"][:, sl(h)], d["bm"][:, sl(h)]],
                             axis=0)
        krs.append(kr)
        aas.append(jnp.where(keep, _dot_nt(kr, bk), 0.0))
    s0s = [state_ref[s, h] for s, h in chains]
    xxs = [_dot_nt(kr, s0) for kr, s0 in zip(krs, s0s)]
    rts = [jnp.where(low, _dot(aa[0:C, 0:C], aa[0:C, 0:2 * C]), eye_hi - aa[0:C, 2 * C:4 * C]) for aa in aas]
    for _ in range(int(math.log2(C)) - 1):
        outs = [_dot(rt[:, 0:C], rt) for rt in rts]
        rts = [jnp.where(low, out, rt + out) for rt, out in zip(rts, outs)]
    vs = [prep[s]["v"][:, sl(h)] for s, h in chains]
    gms = [xx[0:C] + _dot(aa[0:C, 2 * C:3 * C], v_h) for xx, aa, v_h in zip(xxs, aas, vs)]
    us = [-_dot(rt[:, C:2 * C], gm) for rt, gm in zip(rts, gms)]
    uvs = [jnp.concatenate([u.astype(BF16), v_h], axis=0) for u, v_h in zip(us, vs)]
    ys = [xx[C:2 * C] + _dot(aa[C:2 * C, 0:2 * C], uv) for xx, aa, uv in zip(xxs, aas, uvs)]
    news = []
    for (s, h), s0, uv in zip(chains, s0s, uvs):
        d = prep[s]
        btkt = jnp.concatenate([d["bt"][:, sl(h)], d["kt"][:, sl(h)]], axis=0)
        news.append(s0 * d["w_c"][:, sl(h)] + _dot_tn(uv, btkt))
    for (s, h), y_h, new in zip(chains, ys, news):
        y_ref[s, :, sl(h)] = y_h
        state_ref[s, h] = new

    for s in range(G):
        d = prep[s]
        y = y_ref[s]
        mean = head_sum(y) * (1.0 / N)
        yc = y - mean
        var = head_sum(yc * yc) * (1.0 / N)
        yn = yc * lax.rsqrt(var + GN_EPS) * gnw_ref[...] + gnb_ref[...]
        o_ref[s] = ((yn + d["bonus"]) * d["gate"]).astype(o_ref.dtype)


def _rwkv(ps, mu, w0, dup, a0, aup, gup, k_k, k_a, r_k, gn_w, gn_b, hsum):
    B, S, _ = ps.shape
    C = RW_CHUNK
    G = RW_SEQS if B % RW_SEQS == 0 else 1
    row = lambda n: pl.BlockSpec((1, n), lambda b, j: (0, 0))
    return pl.pallas_call(
        _rwkv_kernel,
        grid=(B // G, S // C),
        in_specs=[
            pl.BlockSpec((G, C, SHIFT_DIM), lambda b, j: (b, j, 0)),
            row(SHIFT_DIM), row(RW_DIM),
            pl.BlockSpec((DECAY_LORA, RW_DIM), lambda b, j: (0, 0)),
            row(RW_DIM),
            pl.BlockSpec((AAA_LORA, RW_DIM), lambda b, j: (0, 0)),
            pl.BlockSpec((GATE_LORA, RW_DIM), lambda b, j: (0, 0)),
            row(RW_DIM), row(RW_DIM), row(RW_DIM), row(RW_DIM), row(RW_DIM),
            pl.BlockSpec((RW_DIM, RW_DIM), lambda b, j: (0, 0)),
        ],
        out_specs=pl.BlockSpec((G, C, RW_DIM), lambda b, j: (b, j, 0)),
        out_shape=jax.ShapeDtypeStruct((B, S, RW_DIM), BF16),
        scratch_shapes=[
            pltpu.VMEM((G, RW_HEADS, RW_HEAD, RW_HEAD), F32),
            pltpu.VMEM((G, 8, SHIFT_DIM), F32),
            pltpu.VMEM((G, C, RW_DIM), F32),
        ],
        compiler_params=_cparams("parallel", "arbitrary"),
        name="rwkv",
    )(ps, mu, w0, dup, a0, aup, gup, k_k, k_a, r_k, gn_w, gn_b, hsum)


def _rope_kernel(pos_ref, cos_ref, sin_ref):
    half = QK_ROPE // 2
    lane = lax.broadcasted_iota(jnp.int32, (1, HEAD_PAD), 1)
    in_rope = (lane >= QK_NOPE) & (lane < QK_NOPE + QK_ROPE)
    fidx = jnp.where(lane < QK_NOPE + half, lane - QK_NOPE, lane - QK_NOPE - half)
    inv_freq = jnp.exp(-math.log(ROPE_THETA) * fidx.astype(F32) / half)
    ang = pos_ref[0] * inv_freq
    cos_ref[0] = jnp.where(in_rope, jnp.cos(ang), 0.0)
    sin_ref[0] = jnp.where(in_rope, jnp.sin(ang), 0.0)


def _rope_tables(posf):
    B, S, _ = posf.shape
    tm = min(TM_PREP, S)
    out = pl.BlockSpec((1, tm, HEAD_PAD), lambda b, i: (b, i, 0))
    return pl.pallas_call(
        _rope_kernel,
        grid=(B, S // tm),
        in_specs=[pl.BlockSpec((1, tm, 1), lambda b, i: (b, i, 0))],
        out_specs=[out, out],
        out_shape=[jax.ShapeDtypeStruct((B, S, HEAD_PAD), F32)] * 2,
        compiler_params=_cparams("parallel", "parallel"),
        name="rope_tables",
    )(posf)


def _mla_prep_kernel(lat_ref, cos_ref, sin_ref, qnw_ref, wq_ref, wqr_ref, kvnw_ref, wk_ref, wv_ref,
                     place_ref, placer_ref, q_ref, k_ref, v_ref):
    lat = lat_ref[0]
    c_q = lat[:, 0:Q_LORA]
    c_kv = lat[:, Q_LORA:Q_LORA + KV_LORA]
    k_rope = lat[:, Q_LORA + KV_LORA:LAT_DIM]

    def rms(z, w):
        return z * lax.rsqrt(jnp.mean(z * z, -1, keepdims=True) + RMS_EPS) * w

    lane = lax.broadcasted_iota(jnp.int32, (1, HEAD_PAD), 1)
    cos_r = cos_ref[0]
    sin_r = sin_ref[0]
    scale = (QK_NOPE + QK_ROPE) ** -0.5
    q_mul = jnp.where(lane < QK_NOPE, 1.0, cos_r) * scale
    q_mul_rot = sin_r * scale

    cq_n = rms(c_q, qnw_ref[...]).astype(BF16)
    q_all = jnp.dot(cq_n, wq_ref[...], preferred_element_type=F32)
    q_rot = jnp.dot(cq_n, wqr_ref[...], preferred_element_type=F32)
    ckv_n = rms(c_kv, kvnw_ref[...]).astype(BF16)
    k_all = jnp.dot(ckv_n, wk_ref[...], preferred_element_type=F32)
    v_all_t = lax.dot_general(wv_ref[...], ckv_n, (((1,), (1,)), ((), ())), preferred_element_type=F32)
    kr_pad = (_dot_split_lhs(k_rope, place_ref[...], 3) * cos_r
              + _dot_split_lhs(k_rope, placer_ref[...], 3) * sin_r)
    for h in range(MLA_HEADS):
        sl = slice(h * HEAD_PAD, (h + 1) * HEAD_PAD)
        q_ref[0, h] = (q_all[:, sl] * q_mul + q_rot[:, sl] * q_mul_rot).astype(q_ref.dtype)
        k_ref[0, h] = (k_all[:, sl] + kr_pad).astype(k_ref.dtype)
    for hp in range(MLA_HEADS // 2):
        v_ref[0, hp] = v_all_t[hp * LANES:(hp + 1) * LANES, :].astype(v_ref.dtype)


def _mla_prep(lat, cos_t, sin_t, qnw, wq, wqr, kvnw, wk, wv, place, placer):
    B, S, _ = lat.shape
    tm = min(TM_PREP, S)
    H = MLA_HEADS
    cst = lambda shape: pl.BlockSpec(shape, lambda b, i: (0,) * len(shape))
    return pl.pallas_call(
        _mla_prep_kernel,
        grid=(B, S // tm),
        in_specs=[
            pl.BlockSpec((1, tm, LAT_PAD), lambda b, i: (b, i, 0)),
            pl.BlockSpec((1, tm, HEAD_PAD), lambda b, i: (b, i, 0)),
            pl.BlockSpec((1, tm, HEAD_PAD), lambda b, i: (b, i, 0)),
            cst((1, Q_LORA)), cst((Q_LORA, H * HEAD_PAD)), cst((Q_LORA, H * HEAD_PAD)),
            cst((1, KV_LORA)), cst((KV_LORA, H * HEAD_PAD)), cst((MLA_DIM, KV_LORA)),
            cst((QK_ROPE, HEAD_PAD)), cst((QK_ROPE, HEAD_PAD)),
        ],
        out_specs=[
            pl.BlockSpec((1, H, tm, HEAD_PAD), lambda b, i: (b, 0, i, 0)),
            pl.BlockSpec((1, H, tm, HEAD_PAD), lambda b, i: (b, 0, i, 0)),
            pl.BlockSpec((1, H // 2, LANES, tm), lambda b, i: (b, 0, 0, i)),
        ],
        out_shape=[
            jax.ShapeDtypeStruct((B, H, S, HEAD_PAD), BF16),
            jax.ShapeDtypeStruct((B, H, S, HEAD_PAD), BF16),
            jax.ShapeDtypeStruct((B, H // 2, LANES, S), BF16),
        ],
        compiler_params=_cparams("parallel", "parallel"),
        name="mla_prep",
    )(lat, cos_t, sin_t, qnw, wq, wqr, kvnw, wk, wv, place, placer)


def _attn_kernel(q_ref, k_ref, vt_ref, o_ref, *, tq):
    i = pl.program_id(2)
    ki = lax.broadcasted_iota(jnp.int32, (tq, tq), 0)
    qi = lax.broadcasted_iota(jnp.int32, (tq, tq), 1)
    row_v = lax.broadcasted_iota(jnp.int32, (LANES, tq), 0)
    one = jnp.ones((), BF16)
    qs = [q_ref[0, hh] for hh in range(2)]

    def step(j, carry, masked):
        start = pl.multiple_of(j * tq, tq)
        vt = vt_ref[0, 0, :, pl.ds(start, tq)]
        v_aug = [jnp.where(row_v < V_HEAD, vt, one), jnp.where(row_v < V_HEAD, one, vt)]
        ss = [lax.dot_general(k_ref[0, hh, pl.ds(start, tq), :], qs[hh], (((1,), (1,)), ((), ())),
                              preferred_element_type=F32) for hh in range(2)]
        new = []
        ps = []
        for hh in range(2):
            m, acc = carry[hh]
            s = ss[hh]
            if masked:
                s = jnp.where(ki <= qi, s, NEG_BIG)
            m_new = jnp.maximum(m, jnp.max(s, 0, keepdims=True))
            ps.append(jnp.exp(s - m_new).astype(BF16))
            new.append((m_new, jnp.exp(m - m_new) * acc))
        pvs = [jnp.dot(v_aug[hh], ps[hh], preferred_element_type=F32) for hh in range(2)]
        return tuple((new[hh][0], new[hh][1] + pvs[hh]) for hh in range(2))

    init1 = (jnp.full((1, tq), NEG_BIG, F32), jnp.zeros((LANES, tq), F32))
    carry = lax.fori_loop(0, i, functools.partial(step, masked=False), (init1, init1))
    carry = step(i, carry, True)
    a0 = carry[0][1]
    a1 = carry[1][1]
    out_t = jnp.concatenate([a0[0:V_HEAD] / a0[V_HEAD:], a1[V_HEAD:] / a1[0:V_HEAD]], axis=0)
    o_ref[0] = out_t.T.astype(o_ref.dtype)


def _attn(q, k, vt):
    B, H, S, _ = q.shape
    tq = min(TQ, S)
    return pl.pallas_call(
        functools.partial(_attn_kernel, tq=tq),
        grid=(B, H // 2, S // tq),
        in_specs=[
            pl.BlockSpec((1, 2, tq, HEAD_PAD), lambda b, p, i: (b, p, i, 0)),
            pl.BlockSpec((1, 2, S, HEAD_PAD), lambda b, p, i: (b, p, 0, 0)),
            pl.BlockSpec((1, 1, LANES, S), lambda b, p, i: (b, p, 0, 0)),
        ],
        out_specs=pl.BlockSpec((1, tq, LANES), lambda b, p, i: (b, i, p)),
        out_shape=jax.ShapeDtypeStruct((B, S, MLA_DIM), BF16),
        compiler_params=_cparams("parallel", "parallel", "arbitrary"),
        name="attn",
    )(q, k, vt)


def _residual_epilogue(x, f, gate, ln_w, ln_b):
    return _layer_norm(ALPHA * x + gate * f) * ln_w + ln_b


def _merge_out_kernel(x_ref, ya_ref, yb_ref, sga_ref, sgb_ref, pa_ref, pb_ref, wo_ref, mod_ref,
                      lnw_ref, lnb_ref, x1_ref, h_ref):
    m = mod_ref[0]
    tm = x_ref.shape[1]
    halves = [slice(0, tm // 2), slice(tm // 2, tm)]
    mas = [jnp.dot(ya_ref[0, r, :], pa_ref[...], preferred_element_type=F32) for r in halves]
    mbs = [jnp.dot(yb_ref[0, r, :], pb_ref[...], preferred_element_type=F32) for r in halves]
    merged = [(sga_ref[0, r, :].astype(F32) * ma + sgb_ref[0, r, :].astype(F32) * mb).astype(BF16)
              for r, ma, mb in zip(halves, mas, mbs)]
    mixes = [jnp.dot(mg, wo_ref[...], preferred_element_type=F32) for mg in merged]
    for r, mix in zip(halves, mixes):
        x1 = _residual_epilogue(x_ref[0, r, :], mix, m[2:3], lnw_ref[...], lnb_ref[...])
        x1_ref[0, r, :] = x1
        h_ref[0, r, :] = (_layer_norm(x1) * (1.0 + m[4:5]) + m[3:4]).astype(h_ref.dtype)


def _merge_out(x, ya, yb, sg, pa, pb, wo, mod, ln_w, ln_b, h_dtype):
    B, S, _ = x.shape
    tm = min(TM_MERGE, S)
    cst = lambda shape: pl.BlockSpec(shape, lambda b, i: (0,) * len(shape))
    rows = lambda n: pl.BlockSpec((1, tm, n), lambda b, i: (b, i, 0))
    return pl.pallas_call(
        _merge_out_kernel,
        grid=(B, S // tm),
        in_specs=[
            rows(D_MODEL), rows(RW_DIM), rows(MLA_DIM),
            pl.BlockSpec((1, tm, D_MODEL), lambda b, i: (b, i, 0)),
            pl.BlockSpec((1, tm, D_MODEL), lambda b, i: (b, i, 1)),
            cst((RW_DIM, D_MODEL)), cst((MLA_DIM, D_MODEL)), cst((D_MODEL, D_MODEL)),
            pl.BlockSpec((1, 6, D_MODEL), lambda b, i: (b, 0, 0)),
            cst((1, D_MODEL)), cst((1, D_MODEL)),
        ],
        out_specs=[rows(D_MODEL), rows(D_MODEL)],
        out_shape=[
            jax.ShapeDtypeStruct((B, S, D_MODEL), F32),
            jax.ShapeDtypeStruct((B, S, D_MODEL), h_dtype),
        ],
        compiler_params=_cparams("parallel", "parallel"),
        name="merge_out",
    )(x, ya, yb, sg, sg, pa, pb, wo, mod, ln_w, ln_b)


def _write_epilogue(x1, f, m, modn_ref, lnw_ref, lnb_ref, x2_ref, u_ref):
    x2 = _residual_epilogue(x1, f, m[5:6], lnw_ref[...], lnb_ref[...])
    x2_ref[0] = x2
    mn = modn_ref[0]
    u_ref[0] = (_layer_norm(x2) * (1.0 + mn[1:2]) + mn[0:1]).astype(u_ref.dtype)


def _ffn_kernel(x1_ref, h_ref, wg_ref, wu_ref, wd_ref, mod_ref, modn_ref, lnw_ref, lnb_ref,
                x2_ref, u_ref):
    h = h_ref[0]
    acc = None
    for j in range(D_FF // FF_CHUNK):
        sl = slice(j * FF_CHUNK, (j + 1) * FF_CHUNK)
        gt = jnp.dot(h, wg_ref[:, sl], preferred_element_type=F32)
        up = jnp.dot(h, wu_ref[:, sl], preferred_element_type=F32)
        act = (gt * jax.nn.sigmoid(gt) * up).astype(BF16)
        t = jnp.dot(act, wd_ref[sl, :], preferred_element_type=F32)
        acc = t if acc is None else acc + t
    _write_epilogue(x1_ref[0], acc, mod_ref[0], modn_ref, lnw_ref, lnb_ref, x2_ref, u_ref)


def _ffn(x1, h, wg, wu, wd, mod, modn, ln_w, ln_b):
    B, S, _ = x1.shape
    tm = min(TM_FFN, S)
    res = lambda shape: pl.BlockSpec(shape, lambda b, i: (0,) * len(shape), pipeline_mode=pl.Buffered(1))
    cst = lambda shape: pl.BlockSpec(shape, lambda b, i: (0,) * len(shape))
    rows = pl.BlockSpec((1, tm, D_MODEL), lambda b, i: (b, i, 0))
    modspec = pl.BlockSpec((1, 6, D_MODEL), lambda b, i: (b, 0, 0))
    return pl.pallas_call(
        _ffn_kernel,
        grid=(B, S // tm),
        in_specs=[rows, rows, res((D_MODEL, D_FF)), res((D_MODEL, D_FF)), res((D_FF, D_MODEL)),
                  modspec, modspec, cst((1, D_MODEL)), cst((1, D_MODEL))],
        out_specs=[rows, rows],
        out_shape=[
            jax.ShapeDtypeStruct((B, S, D_MODEL), F32),
            jax.ShapeDtypeStruct((B, S, D_MODEL), BF16),
        ],
        compiler_params=_cparams("parallel", "parallel"),
        name="ffn",
    )(x1, h, wg, wu, wd, mod, modn, ln_w, ln_b)


def _router_kernel(h_ref, rw_ref, rb_ref, ltri_ref, rec_ref, cnt_ref, carry_ref):
    first = (pl.program_id(0) == 0) & (pl.program_id(1) == 0)

    @pl.when(first)
    def _init():
        carry_ref[...] = jnp.zeros_like(carry_ref)

    h = h_ref[0]
    tm = h.shape[0]
    logits = jnp.dot(h, rw_ref[...], preferred_element_type=F32, precision=_HI) + rb_ref[...]
    lane = lax.broadcasted_iota(jnp.int32, (tm, LANES), 1)
    m1 = jnp.max(logits, -1, keepdims=True)
    i1 = jnp.min(jnp.where(logits == m1, lane, LANES), -1, keepdims=True)
    rest = jnp.where(lane == i1, -jnp.inf, logits)
    m2 = jnp.max(rest, -1, keepdims=True)
    i2 = jnp.min(jnp.where(rest == m2, lane, LANES), -1, keepdims=True)
    e21 = jnp.exp(m2 - m1)
    g1 = 1.0 / (1.0 + e21)
    g2 = e21 / (1.0 + e21)
    oh1 = lane == i1
    oh2 = lane == i2
    cnt = oh1.astype(F32) + oh2.astype(F32)
    before = jnp.dot(ltri_ref[...], cnt.astype(BF16), preferred_element_type=F32) + carry_ref[0:1, :]
    rank1 = jnp.sum(jnp.where(oh1, before, 0.0), -1, keepdims=True)
    rank2 = jnp.sum(jnp.where(oh2, before, 0.0), -1, keepdims=True)
    carry_ref[0:1, :] = carry_ref[0:1, :] + jnp.sum(cnt, 0, keepdims=True)
    cols = (i1.astype(F32), i2.astype(F32), g1, g2, rank1, rank2)
    rec = jnp.zeros((tm, LANES), F32)
    for c, val in enumerate(cols):
        rec = jnp.where(lane == c, val, rec)
    rec_ref[0] = rec
    cnt_ref[...] = carry_ref[...]


def _router(h, rw_pad, rb_pad, ltri):
    B, S, _ = h.shape
    tm = min(TM_ROUTE, S)
    return pl.pallas_call(
        _router_kernel,
        grid=(B, S // tm),
        in_specs=[
            pl.BlockSpec((1, tm, D_MODEL), lambda b, i: (b, i, 0)),
            pl.BlockSpec((D_MODEL, LANES), lambda b, i: (0, 0)),
            pl.BlockSpec((1, LANES), lambda b, i: (0, 0)),
            pl.BlockSpec((tm, tm), lambda b, i: (0, 0)),
        ],
        out_specs=[
            pl.BlockSpec((1, tm, LANES), lambda b, i: (b, i, 0)),
            pl.BlockSpec((8, LANES), lambda b, i: (0, 0)),
        ],
        out_shape=[
            jax.ShapeDtypeStruct((B, S, LANES), F32),
            jax.ShapeDtypeStruct((8, LANES), F32),
        ],
        scratch_shapes=[pltpu.VMEM((8, LANES), F32)],
        compiler_params=_cparams("arbitrary", "arbitrary"),
        name="router",
    )(h, rw_pad, rb_pad, ltri)


def _expert_kernel(be_ref, nu_ref, tok_ref, h_hbm, wg_ref, wu_ref, wd_ref, ys_ref,
                   rows_ref, x_ref, sems, *, tm):
    i = pl.program_id(0)
    j = pl.program_id(1)
    nb = pl.num_programs(0)
    nj = pl.num_programs(1)
    per_step = tm // EXP_NJ
    used = i < nu_ref[0]

    def row(slot_index, slot, part, r):
        return pltpu.make_async_copy(h_hbm.at[pl.ds(tok_ref[slot_index], 1)],
                                     rows_ref.at[slot, part, pl.ds(r, 1)], sems.at[slot])

    def wait_block(slot):
        def wait(r, _):
            row(0, slot, 0, 0).wait()
            return 0
        lax.fori_loop(0, tm, wait, 0, unroll=8)

    @pl.when((i == 0) & (j == 0))
    def _first():
        for part in range(EXP_NJ):
            def start(r, _):
                row(part * per_step + r, 0, part, r).start()
                return 0
            lax.fori_loop(0, per_step, start, 0, unroll=8)

    @pl.when(j == 0)
    def _arrive():
        wait_block(i % 2)
        for part in range(EXP_NJ):
            x_ref[part * per_step:(part + 1) * per_step, :] = rows_ref[i % 2, part].astype(BF16)
        nxt_base = jnp.minimum(i + 1, nb - 1) * tm
        for part in range(EXP_NJ):
            for r in range(per_step):
                row(nxt_base + part * per_step + r, (i + 1) % 2, part, r).start()

    x = x_ref[...]
    acc = None
    for c in range(wg_ref.shape[2] // FF_CHUNK):
        cols = slice(c * FF_CHUNK, (c + 1) * FF_CHUNK)
        gt = jnp.dot(x, wg_ref[0, :, cols], preferred_element_type=F32)
        up = jnp.dot(x, wu_ref[0, :, cols], preferred_element_type=F32)
        act = (gt * jax.nn.sigmoid(gt) * up).astype(BF16)
        t = jnp.dot(act, wd_ref[0, cols, :], preferred_element_type=F32)
        acc = t if acc is None else acc + t
    acc = jnp.where(used, acc, 0.0)

    @pl.when(j == 0)
    def _():
        ys_ref[...] = acc

    @pl.when(j > 0)
    def _():
        ys_ref[...] += acc

    @pl.when((i == nb - 1) & (j == nj - 1))
    def _drain():
        wait_block((i + 1) % 2)


def _experts(block_e, n_used, slot_tok, h2d, wg, wu, wd):
    m_pad = slot_tok.shape[0]
    tm = TM_EXP
    nj = EXP_NJ
    tf = D_FF_EXPERT // nj

    def jj(i, j, nu):
        return jnp.where(i < nu[0], j, nj - 1)

    return pl.pallas_call(
        functools.partial(_expert_kernel, tm=tm),
        grid_spec=pltpu.PrefetchScalarGridSpec(
            num_scalar_prefetch=3,
            grid=(m_pad // tm, nj),
            in_specs=[
                pl.BlockSpec(memory_space=pl.ANY),
                pl.BlockSpec((1, D_MODEL, tf), lambda i, j, be, nu, tok: (be[i], 0, jj(i, j, nu))),
                pl.BlockSpec((1, D_MODEL, tf), lambda i, j, be, nu, tok: (be[i], 0, jj(i, j, nu))),
                pl.BlockSpec((1, tf, D_MODEL), lambda i, j, be, nu, tok: (be[i], jj(i, j, nu), 0)),
            ],
            out_specs=pl.BlockSpec((tm, D_MODEL), lambda i, j, be, nu, tok: (i, 0)),
            scratch_shapes=[pltpu.VMEM((2, nj, tm // nj, D_MODEL), F32), pltpu.VMEM((tm, D_MODEL), BF16),
                            pltpu.SemaphoreType.DMA((2,))],
        ),
        out_shape=jax.ShapeDtypeStruct((m_pad, D_MODEL), F32),
        compiler_params=_cparams("arbitrary", "arbitrary"),
        name="experts",
    )(block_e, n_used, slot_tok, h2d, wg, wu, wd)


def _combine_kernel(dest_ref, x1_ref, gate_ref, ys_hbm, mod_ref, modn_ref, lnw_ref, lnb_ref,
                    x2_ref, u_ref, buf_ref, sems, *, tm):
    n_inner = pl.num_programs(1)
    tile = pl.program_id(0) * n_inner + pl.program_id(1)
    n_tiles = pl.num_programs(0) * n_inner
    sub = buf_ref.shape[3]
    groups = tm // sub
    slot = tile % 2

    def row(tok0, slot_, group, u, k):
        t = tok0 + group * sub + u
        return pltpu.make_async_copy(ys_hbm.at[pl.ds(dest_ref[2 * t + k], 1)],
                                     buf_ref.at[slot_, k, group, pl.ds(u, 1)], sems.at[slot_])

    def wait_tile(slot_):
        def wait(group, _):
            for u in range(sub):
                row(0, slot_, 0, 0, 0).wait()
                row(0, slot_, 0, 0, 1).wait()
            return 0
        lax.fori_loop(0, groups, wait, 0)

    @pl.when(tile == 0)
    def _first():
        def start(group, _):
            for u in range(sub):
                row(0, 0, group, u, 0).start()
                row(0, 0, group, u, 1).start()
            return 0
        lax.fori_loop(0, groups, start, 0)

    wait_tile(slot)
    nxt_tok0 = jnp.minimum(tile + 1, n_tiles - 1) * tm
    for group in range(groups):
        for u in range(sub):
            row(nxt_tok0, 1 - slot, group, u, 0).start()
            row(nxt_tok0, 1 - slot, group, u, 1).start()
    gate = gate_ref[0]
    f = (gate[:, 0:1] * buf_ref[slot, 0].reshape(tm, D_MODEL)
         + gate[:, 1:2] * buf_ref[slot, 1].reshape(tm, D_MODEL))
    _write_epilogue(x1_ref[0], f, mod_ref[0], modn_ref, lnw_ref, lnb_ref, x2_ref, u_ref)

    @pl.when(tile == n_tiles - 1)
    def _drain():
        wait_tile(1 - slot)


def _combine(dest_flat, x1, gates, ys, mod, modn, ln_w, ln_b):
    B, S, _ = x1.shape
    tm = min(TM_COMB, S)
    rows = pl.BlockSpec((1, tm, D_MODEL), lambda b, i, d: (b, i, 0))
    modspec = pl.BlockSpec((1, 6, D_MODEL), lambda b, i, d: (b, 0, 0))
    cst = pl.BlockSpec((1, D_MODEL), lambda b, i, d: (0, 0))
    return pl.pallas_call(
        functools.partial(_combine_kernel, tm=tm),
        grid_spec=pltpu.PrefetchScalarGridSpec(
            num_scalar_prefetch=1,
            grid=(B, S // tm),
            in_specs=[rows, pl.BlockSpec((1, tm, TOP_K), lambda b, i, d: (b, i, 0)),
                      pl.BlockSpec(memory_space=pl.ANY), modspec, modspec, cst, cst],
            out_specs=[rows, rows],
            scratch_shapes=[pltpu.VMEM((2, TOP_K, tm // SUBLANES, SUBLANES, D_MODEL), F32),
                            pltpu.SemaphoreType.DMA((2,))],
        ),
        out_shape=[
            jax.ShapeDtypeStruct((B, S, D_MODEL), F32),
            jax.ShapeDtypeStruct((B, S, D_MODEL), BF16),
        ],
        compiler_params=_cparams("arbitrary", "arbitrary"),
        name="combine",
    )(dest_flat, x1, gates, ys, mod, modn, ln_w, ln_b)


def _moe(x1, h, rw_pad, rb_pad, ltri, wg, wu, wd, mod, modn, ln_w, ln_b):
    B, S, _ = x1.shape
    n = B * S
    m = n * TOP_K
    rec, cnt = _router(h, rw_pad, rb_pad, ltri)
    rec = rec.reshape(n, LANES)
    e = rec[:, 0:TOP_K].astype(jnp.int32)
    gates = rec[:, TOP_K:2 * TOP_K]
    rank = rec[:, 2 * TOP_K:3 * TOP_K].astype(jnp.int32)
    counts = cnt[0, :N_EXPERTS].astype(jnp.int32)
    padded = (counts + TM_EXP - 1) // TM_EXP * TM_EXP
    pad_end = jnp.cumsum(padded)
    pad_start = pad_end - padded
    dest = (pad_start[e] + rank).reshape(m)
    n_blocks = m // TM_EXP + N_EXPERTS
    block_e = jnp.minimum(
        jnp.searchsorted(pad_end, jnp.arange(n_blocks, dtype=jnp.int32) * TM_EXP, side="right"),
        N_EXPERTS - 1).astype(jnp.int32)
    n_used = (pad_end[-1:] // TM_EXP).astype(jnp.int32)
    slot_tok = jnp.zeros((n_blocks * TM_EXP,), jnp.int32).at[dest].set(
        jnp.arange(m, dtype=jnp.int32) // TOP_K)
    ys = _experts(block_e, n_used, slot_tok, h.reshape(n, D_MODEL), wg, wu, wd)
    return _combine(dest, x1, gates.reshape(B, S, TOP_K), ys, mod, modn, ln_w, ln_b)


def _mla_weights(w_uq, w_ukv):
    L = w_uq.shape[0]
    H = MLA_HEADS
    half = QK_ROPE // 2
    wq = w_uq.reshape(L, Q_LORA, H, QK_NOPE + QK_ROPE)
    nope, rope = wq[..., :QK_NOPE], wq[..., QK_NOPE:]
    zpad = jnp.zeros((L, Q_LORA, H, HEAD_PAD - QK_NOPE - QK_ROPE), w_uq.dtype)
    wq_pad = jnp.concatenate([nope, rope, zpad], -1).reshape(L, Q_LORA, H * HEAD_PAD)
    rope_rot = jnp.concatenate([-rope[..., half:], rope[..., :half]], -1)
    wq_rot = jnp.concatenate([jnp.zeros_like(nope), rope_rot, zpad], -1).reshape(L, Q_LORA, H * HEAD_PAD)
    wkv = w_ukv.reshape(L, KV_LORA, H, QK_NOPE + V_HEAD)
    k_nope, v = wkv[..., :QK_NOPE], wkv[..., QK_NOPE:]
    wk_pad = jnp.concatenate(
        [k_nope, jnp.zeros((L, KV_LORA, H, HEAD_PAD - QK_NOPE), w_ukv.dtype)], -1
    ).reshape(L, KV_LORA, H * HEAD_PAD)
    wv = jnp.swapaxes(v.reshape(L, KV_LORA, MLA_DIM), 1, 2)
    return wq_pad.astype(BF16), wq_rot.astype(BF16), wk_pad.astype(BF16), wv.astype(BF16)


def _rope_placement():
    half = QK_ROPE // 2
    src = jnp.arange(QK_ROPE)[:, None]
    dst = jnp.arange(HEAD_PAD)[None, :]
    place = (dst == src + QK_NOPE).astype(BF16)
    rot = jnp.where((src < half) & (dst == src + QK_NOPE + half), 1.0,
                    jnp.where((src >= half) & (dst == src + QK_NOPE - half), -1.0, 0.0))
    return place, rot.astype(BF16)


def kernel(x, c, positions, w_ada, b_ada, w_in, shift_mu, decay_w0, decay_up, aaa_a0, aaa_up, gate_up, k_k, k_a, r_k, gn_w, gn_b, q_norm_w, w_uq, kv_norm_w, w_ukv, p_a, p_b, w_o, ln1_w, ln1_b, ln2_w, ln2_b, ffn_w_gate, ffn_w_up, ffn_w_down, router_w, router_b, moe_w_gate, moe_w_up, moe_w_down):
    B, S, _ = x.shape
    n = B * S
    L = DEPTH
    assert S % RW_CHUNK == 0 and S % LANES == 0

    mod = _ada(c, w_ada, b_ada).reshape(L, B, 6, D_MODEL)
    cos_t, sin_t = _rope_tables(positions.astype(F32).reshape(B, S, 1))

    w_shift = w_in[:, :, :SHIFT_DIM].astype(BF16)
    w_lat = jnp.pad(w_in[:, :, SHIFT_DIM:SHIFT_DIM + LAT_DIM],
                    ((0, 0), (0, 0), (0, LAT_PAD - LAT_DIM))).astype(BF16)
    w_gates = w_in[:, :, SHIFT_DIM + LAT_DIM:].astype(BF16)
    wq_pad, wq_rot, wk_pad, wv = _mla_weights(w_uq, w_ukv)
    place, placer = _rope_placement()
    lane_head = jnp.arange(RW_DIM) // RW_HEAD
    hsum = (lane_head[:, None] == lane_head[None, :]).astype(BF16)
    row1 = lambda a, l: a[l].reshape(1, -1)
    tm_r = min(TM_ROUTE, S)
    ltri = (jnp.arange(tm_r)[None, :] < jnp.arange(tm_r)[:, None]).astype(BF16)
    rw_pad = jnp.pad(router_w, ((0, 0), (0, 0), (0, LANES - N_EXPERTS)))
    rb_pad = jnp.pad(router_b, ((0, 0), (0, LANES - N_EXPERTS)), constant_values=NEG_BIG)

    u = _lnmod(x, mod[0])
    for l in range(L):
        u2 = u.reshape(n, D_MODEL)
        ps, lat, sg = _proj(u2, w_shift[l], w_lat[l], w_gates[l])
        ps = ps.reshape(B, S, SHIFT_DIM)
        lat = lat.reshape(B, S, LAT_PAD)
        sg = sg.reshape(B, S, 2 * D_MODEL)
        ya = _rwkv(ps, row1(shift_mu, l), row1(decay_w0, l), decay_up[l].astype(BF16), row1(aaa_a0, l),
                   aaa_up[l].astype(BF16), gate_up[l].astype(BF16), row1(k_k, l), row1(k_a, l),
                   row1(r_k, l), row1(gn_w, l), row1(gn_b, l), hsum)
        q, k, v = _mla_prep(lat, cos_t, sin_t, row1(q_norm_w, l), wq_pad[l], wq_rot[l], row1(kv_norm_w, l),
                            wk_pad[l], wv[l], place, placer)
        yb = _attn(q, k, v)
        moe_layer = l % 2 == 1
        x1, h = _merge_out(x, ya, yb, sg, p_a[l].astype(BF16), p_b[l].astype(BF16), w_o[l].astype(BF16),
                           mod[l], row1(ln1_w, l), row1(ln1_b, l), F32 if moe_layer else BF16)
        modn = mod[min(l + 1, L - 1)]
        if moe_layer:
            x, u = _moe(x1, h, rw_pad[l // 2], rb_pad[l // 2].reshape(1, LANES), ltri,
                        moe_w_gate[l // 2].astype(BF16), moe_w_up[l // 2].astype(BF16),
                        moe_w_down[l // 2].astype(BF16), mod[l], modn, row1(ln2_w, l), row1(ln2_b, l))
        else:
            x, u = _ffn(x1, h, ffn_w_gate[l // 2].astype(BF16), ffn_w_up[l // 2].astype(BF16),
                        ffn_w_down[l // 2].astype(BF16), mod[l], modn, row1(ln2_w, l), row1(ln2_b, l))
    return x
```

```python
import functools
import math

import jax
import jax.numpy as jnp
from jax import lax
from jax.experimental import pallas as pl
from jax.experimental.pallas import tpu as pltpu

F32 = jnp.float32
BF16 = jnp.bfloat16

D_MODEL = 1024
DEPTH = 4
RW_HEADS = 8
RW_HEAD = 64
RW_DIM = RW_HEADS * RW_HEAD
DECAY_LORA = 64
AAA_LORA = 64
GATE_LORA = 128
GN_EPS = 64e-5
MLA_HEADS = 8
QK_NOPE = 64
QK_ROPE = 32
V_HEAD = 64
Q_LORA = 384
KV_LORA = 256
MLA_DIM = MLA_HEADS * V_HEAD
ROPE_THETA = 10000.0
SHIFT_DIM = 3 * RW_DIM + DECAY_LORA + AAA_LORA + GATE_LORA
LAT_DIM = Q_LORA + KV_LORA + QK_ROPE
LAT_PAD = 768
D_FF = 2816
N_EXPERTS = 8
TOP_K = 2
D_FF_EXPERT = 3584
ALPHA = (2.0 * DEPTH) ** 0.25
LN_EPS = 1e-5
RMS_EPS = 1e-6
LANES = 128
SUBLANES = 8
HEAD_PAD = 128
NEG_BIG = -1e30

RW_CHUNK = 64
RW_SEQS = 4
TM_MM = 512
TM_PREP = 256
TQ = 1024
TM_MERGE = 512
TM_FFN = 512
FF_CHUNK = 256
TM_ROUTE = 512
TM_EXP = 512
EXP_NJ = 2
TM_COMB = 256

_HI = lax.Precision.HIGHEST


def _cparams(*sem):
    return pltpu.CompilerParams(dimension_semantics=sem)


def _dot(a, b):
    return jnp.dot(a.astype(BF16), b.astype(BF16), preferred_element_type=F32)


def _dot_nt(a, b):
    return lax.dot_general(a.astype(BF16), b.astype(BF16), (((1,), (1,)), ((), ())),
                           preferred_element_type=F32)


def _dot_tn(a, b):
    return lax.dot_general(a.astype(BF16), b.astype(BF16), (((0,), (0,)), ((), ())),
                           preferred_element_type=F32)


def _split(x, n):
    parts = []
    for _ in range(n - 1):
        p = x.astype(BF16)
        parts.append(p)
        x = x - p.astype(F32)
    parts.append(x.astype(BF16))
    return parts


def _dot_split_lhs(x, w_bf16, n):
    acc = None
    for p in _split(x, n):
        t = jnp.dot(p, w_bf16, preferred_element_type=F32)
        acc = t if acc is None else acc + t
    return acc


def _dot_split_rhs(w_bf16, x, n):
    acc = None
    for p in _split(x, n):
        t = jnp.dot(w_bf16, p, preferred_element_type=F32)
        acc = t if acc is None else acc + t
    return acc


def _layer_norm(x):
    mu = jnp.mean(x, -1, keepdims=True)
    xc = x - mu
    var = jnp.mean(xc * xc, -1, keepdims=True)
    return xc * lax.rsqrt(var + LN_EPS)


def _full(shape):
    nd = len(shape)
    return pl.BlockSpec(shape, lambda *_: (0,) * nd)


def _ada_kernel(c_ref, w_ref, b_ref, o_ref):
    c = c_ref[...]
    ca = c * jax.nn.sigmoid(c)
    o_ref[0] = jnp.dot(ca, w_ref[0], preferred_element_type=F32, precision=_HI) + b_ref[0]


def _ada(c, w_ada, b_ada):
    L, _, six_d = w_ada.shape
    B = c.shape[0]
    nj = six_d // D_MODEL
    return pl.pallas_call(
        _ada_kernel,
        grid=(L, nj),
        in_specs=[
            pl.BlockSpec((B, D_MODEL), lambda l, j: (0, 0)),
            pl.BlockSpec((1, D_MODEL, D_MODEL), lambda l, j: (l, 0, j)),
            pl.BlockSpec((1, 1, D_MODEL), lambda l, j: (l, 0, j)),
        ],
        out_specs=pl.BlockSpec((1, B, D_MODEL), lambda l, j: (l, 0, j)),
        out_shape=jax.ShapeDtypeStruct((L, B, six_d), F32),
        compiler_params=_cparams("parallel", "parallel"),
        name="ada",
    )(c, w_ada, b_ada.reshape(L, 1, six_d))


def _lnmod_kernel(x_ref, mod_ref, o_ref):
    m = mod_ref[0]
    o_ref[0] = (_layer_norm(x_ref[0]) * (1.0 + m[1:2]) + m[0:1]).astype(o_ref.dtype)


def _lnmod(x, mod):
    B, S, _ = x.shape
    tm = min(TM_MM, S)
    return pl.pallas_call(
        _lnmod_kernel,
        grid=(B, S // tm),
        in_specs=[
            pl.BlockSpec((1, tm, D_MODEL), lambda b, i: (b, i, 0)),
            pl.BlockSpec((1, 6, D_MODEL), lambda b, i: (b, 0, 0)),
        ],
        out_specs=pl.BlockSpec((1, tm, D_MODEL), lambda b, i: (b, i, 0)),
        out_shape=jax.ShapeDtypeStruct((B, S, D_MODEL), BF16),
        compiler_params=_cparams("parallel", "parallel"),
        name="lnmod",
    )(x, mod)


def _proj_kernel(u_ref, ws_ref, wl_ref, wg_ref, ps_ref, lat_ref, sg_ref):
    u = u_ref[...]
    ps_ref[...] = jnp.dot(u, ws_ref[...], preferred_element_type=F32)
    lat_ref[...] = jnp.dot(u, wl_ref[...], preferred_element_type=F32)
    sg_ref[...] = jax.nn.sigmoid(jnp.dot(u, wg_ref[...], preferred_element_type=F32)).astype(sg_ref.dtype)


def _proj(u2, w_shift, w_lat, w_gates):
    n = u2.shape[0]
    tm = min(TM_MM, n)
    res = lambda w: pl.BlockSpec(w.shape, lambda i: (0, 0), pipeline_mode=pl.Buffered(1))
    rows = lambda c: pl.BlockSpec((tm, c), lambda i: (i, 0))
    return pl.pallas_call(
        _proj_kernel,
        grid=(n // tm,),
        in_specs=[rows(D_MODEL), res(w_shift), res(w_lat), res(w_gates)],
        out_specs=[rows(SHIFT_DIM), rows(LAT_PAD), rows(2 * D_MODEL)],
        out_shape=[
            jax.ShapeDtypeStruct((n, SHIFT_DIM), F32),
            jax.ShapeDtypeStruct((n, LAT_PAD), F32),
            jax.ShapeDtypeStruct((n, 2 * D_MODEL), BF16),
        ],
        compiler_params=_cparams("parallel"),
        name="proj",
    )(u2, w_shift, w_lat, w_gates)


def _rwkv_kernel(ps_ref, mu_ref, w0_ref, dup_ref, a0_ref, aup_ref, gup_ref, kk_ref, ka_ref, rk_ref,
                 gnw_ref, gnb_ref, hsum_ref, o_ref, state_ref, prev_ref, y_ref):
    C = RW_CHUNK
    N = RW_HEAD
    G = ps_ref.shape[0]

    @pl.when(pl.program_id(1) == 0)
    def _init():
        state_ref[...] = jnp.zeros_like(state_ref)
        prev_ref[...] = jnp.zeros_like(prev_ref)

    half = RW_DIM // 2
    hs_lo = hsum_ref[0:half, 0:half]
    hs_hi = hsum_ref[half:, half:]

    def head_sum(z):
        zb = z.astype(BF16)
        return jnp.concatenate([jnp.dot(zb[:, 0:half], hs_lo, preferred_element_type=F32),
                                jnp.dot(zb[:, half:], hs_hi, preferred_element_type=F32)], axis=1)

    ri = lax.broadcasted_iota(jnp.int32, (C, C), 0)
    ci = lax.broadcasted_iota(jnp.int32, (C, C), 1)
    tri = (ci <= ri).astype(BF16)
    r4 = lax.broadcasted_iota(jnp.int32, (2 * C, 4 * C), 0)
    c4 = lax.broadcasted_iota(jnp.int32, (2 * C, 4 * C), 1)
    keep = (c4 % C) < (r4 % C) + (r4 >= C).astype(jnp.int32)
    r1 = lax.broadcasted_iota(jnp.int32, (C, 2 * C), 0)
    c1 = lax.broadcasted_iota(jnp.int32, (C, 2 * C), 1)
    low = c1 < C
    eye_hi = (c1 == r1 + C).astype(F32)

    prep = []
    for s in range(G):
        p = ps_ref[s]
        row = lax.broadcasted_iota(jnp.int32, p.shape, 0)
        shifted = jnp.where(row == 0, prev_ref[s, 0:1, :], pltpu.roll(p, 1, 0))
        prev_ref[s, 0:1, :] = p[C - 1:C, :]
        xs = p + (shifted - p) * mu_ref[...]
        r = xs[:, 0:RW_DIM]
        k = xs[:, RW_DIM:2 * RW_DIM]
        v = xs[:, 2 * RW_DIM:3 * RW_DIM]
        o = 3 * RW_DIM
        dw = xs[:, o:o + DECAY_LORA]
        da = xs[:, o + DECAY_LORA:o + DECAY_LORA + AAA_LORA]
        dg = xs[:, o + DECAY_LORA + AAA_LORA:SHIFT_DIM]
        lw = -math.exp(-0.5) * jax.nn.sigmoid(w0_ref[...] + _dot(jnp.tanh(dw), dup_ref[...]))
        a = jax.nn.sigmoid(a0_ref[...] + _dot(da, aup_ref[...]))
        gate = _dot(jax.nn.sigmoid(dg), gup_ref[...])
        kk = k * kk_ref[...]
        kk = kk * lax.rsqrt(jnp.maximum(head_sum(kk * kk), 1e-24))
        k_mod = k * (1.0 + (a - 1.0) * ka_ref[...])
        b = kk * a
        bonus = head_sum(r * k_mod * rk_ref[...]) * v
        cw = _dot_split_rhs(tri, lw, 2)
        cw_last = cw[C - 1:C, :]
        e_neg = jnp.exp(-cw)
        e_rem = jnp.exp(cw_last - cw)
        prep.append(dict(
            kp=(kk * jnp.exp(cw - lw)).astype(BF16), rp=(r * jnp.exp(cw)).astype(BF16),
            bm=(b * e_neg).astype(BF16), km=(k_mod * e_neg).astype(BF16),
            bt=(b * e_rem).astype(BF16), kt=(k_mod * e_rem).astype(BF16),
            v=v.astype(BF16), w_c=jnp.exp(cw_last), bonus=bonus, gate=gate))

    chains = [(s, h) for s in range(G) for h in range(RW_HEADS)]
    sl = lambda h: slice(h * N, (h + 1) * N)

    krs, aas = [], []
    for s, h in chains:
        d = prep[s]
        kr = jnp.concatenate([d["kp"][:, sl(h)], d["rp"][:, sl(h)]], axis=0)
        bk = jnp.concatenate([d["bm"][:, sl(h)], d["km"][:, sl(h)], d["km"][:, sl(h)], d["bm"][:, sl(h)]],
                             axis=0)
        krs.append(kr)
        aas.append(jnp.where(keep, _dot_nt(kr, bk), 0.0))
    s0s = [state_ref[s, h] for s, h in chains]
    xxs = [_dot_nt(kr, s0) for kr, s0 in zip(krs, s0s)]
    rts = [jnp.where(low, _dot(aa[0:C, 0:C], aa[0:C, 0:2 * C]), eye_hi - aa[0:C, 2 * C:4 * C]) for aa in aas]
    for _ in range(int(math.log2(C)) - 1):
        outs = [_dot(rt[:, 0:C], rt) for rt in rts]
        rts = [jnp.where(low, out, rt + out) for rt, out in zip(rts, outs)]
    vs = [prep[s]["v"][:, sl(h)] for s, h in chains]
    gms = [xx[0:C] + _dot(aa[0:C, 2 * C:3 * C], v_h) for xx, aa, v_h in zip(xxs, aas, vs)]
    us = [-_dot(rt[:, C:2 * C], gm) for rt, gm in zip(rts, gms)]
    uvs = [jnp.concatenate([u.astype(BF16), v_h], axis=0) for u, v_h in zip(us, vs)]
    ys = [xx[C:2 * C] + _dot(aa[C:2 * C, 0:2 * C], uv) for xx, aa, uv in zip(xxs, aas, uvs)]
    news = []
    for (s, h), s0, uv in zip(chains, s0s, uvs):
        d = prep[s]
        btkt = jnp.concatenate([d["bt"][:, sl(h)], d["kt"][:, sl(h)]], axis=0)
        news.append(s0 * d["w_c"][:, sl(h)] + _dot_tn(uv, btkt))
    for (s, h), y_h, new in zip(chains, ys, news):
        y_ref[s, :, sl(h)] = y_h
        state_ref[s, h] = new

    for s in range(G):
        d = prep[s]
        y = y_ref[s]
        mean = head_sum(y) * (1.0 / N)
        yc = y - mean
        var = head_sum(yc * yc) * (1.0 / N)
        yn = yc * lax.rsqrt(var + GN_EPS) * gnw_ref[...] + gnb_ref[...]
        o_ref[s] = ((yn + d["bonus"]) * d["gate"]).astype(o_ref.dtype)


def _rwkv(ps, mu, w0, dup, a0, aup, gup, k_k, k_a, r_k, gn_w, gn_b, hsum):
    B, S, _ = ps.shape
    C = RW_CHUNK
    G = RW_SEQS if B % RW_SEQS == 0 else 1
    row = lambda n: pl.BlockSpec((1, n), lambda b, j: (0, 0))
    return pl.pallas_call(
        _rwkv_kernel,
        grid=(B // G, S // C),
        in_specs=[
            pl.BlockSpec((G, C, SHIFT_DIM), lambda b, j: (b, j, 0)),
            row(SHIFT_DIM), row(RW_DIM),
            pl.BlockSpec((DECAY_LORA, RW_DIM), lambda b, j: (0, 0)),
            row(RW_DIM),
            pl.BlockSpec((AAA_LORA, RW_DIM), lambda b, j: (0, 0)),
            pl.BlockSpec((GATE_LORA, RW_DIM), lambda b, j: (0, 0)),
            row(RW_DIM), row(RW_DIM), row(RW_DIM), row(RW_DIM), row(RW_DIM),
            pl.BlockSpec((RW_DIM, RW_DIM), lambda b, j: (0, 0)),
        ],
        out_specs=pl.BlockSpec((G, C, RW_DIM), lambda b, j: (b, j, 0)),
        out_shape=jax.ShapeDtypeStruct((B, S, RW_DIM), BF16),
        scratch_shapes=[
            pltpu.VMEM((G, RW_HEADS, RW_HEAD, RW_HEAD), F32),
            pltpu.VMEM((G, 8, SHIFT_DIM), F32),
            pltpu.VMEM((G, C, RW_DIM), F32),
        ],
        compiler_params=_cparams("parallel", "arbitrary"),
        name="rwkv",
    )(ps, mu, w0, dup, a0, aup, gup, k_k, k_a, r_k, gn_w, gn_b, hsum)


def _rope_kernel(pos_ref, cos_ref, sin_ref):
    half = QK_ROPE // 2
    lane = lax.broadcasted_iota(jnp.int32, (1, HEAD_PAD), 1)
    in_rope = (lane >= QK_NOPE) & (lane < QK_NOPE + QK_ROPE)
    fidx = jnp.where(lane < QK_NOPE + half, lane - QK_NOPE, lane - QK_NOPE - half)
    inv_freq = jnp.exp(-math.log(ROPE_THETA) * fidx.astype(F32) / half)
    ang = pos_ref[0] * inv_freq
    cos_ref[0] = jnp.where(in_rope, jnp.cos(ang), 0.0)
    sin_ref[0] = jnp.where(in_rope, jnp.sin(ang), 0.0)


def _rope_tables(posf):
    B, S, _ = posf.shape
    tm = min(TM_PREP, S)
    out = pl.BlockSpec((1, tm, HEAD_PAD), lambda b, i: (b, i, 0))
    return pl.pallas_call(
        _rope_kernel,
        grid=(B, S // tm),
        in_specs=[pl.BlockSpec((1, tm, 1), lambda b, i: (b, i, 0))],
        out_specs=[out, out],
        out_shape=[jax.ShapeDtypeStruct((B, S, HEAD_PAD), F32)] * 2,
        compiler_params=_cparams("parallel", "parallel"),
        name="rope_tables",
    )(posf)


def _mla_prep_kernel(lat_ref, cos_ref, sin_ref, qnw_ref, wq_ref, wqr_ref, kvnw_ref, wk_ref, wv_ref,
                     place_ref, placer_ref, q_ref, k_ref, v_ref):
    lat = lat_ref[0]
    c_q = lat[:, 0:Q_LORA]
    c_kv = lat[:, Q_LORA:Q_LORA + KV_LORA]
    k_rope = lat[:, Q_LORA + KV_LORA:LAT_DIM]

    def rms(z, w):
        return z * lax.rsqrt(jnp.mean(z * z, -1, keepdims=True) + RMS_EPS) * w

    lane = lax.broadcasted_iota(jnp.int32, (1, HEAD_PAD), 1)
    cos_r = cos_ref[0]
    sin_r = sin_ref[0]
    scale = (QK_NOPE + QK_ROPE) ** -0.5
    q_mul = jnp.where(lane < QK_NOPE, 1.0, cos_r) * scale
    q_mul_rot = sin_r * scale

    cq_n = rms(c_q, qnw_ref[...]).astype(BF16)
    q_all = jnp.dot(cq_n, wq_ref[...], preferred_element_type=F32)
    q_rot = jnp.dot(cq_n, wqr_ref[...], preferred_element_type=F32)
    ckv_n = rms(c_kv, kvnw_ref[...]).astype(BF16)
    k_all = jnp.dot(ckv_n, wk_ref[...], preferred_element_type=F32)
    v_all_t = lax.dot_general(wv_ref[...], ckv_n, (((1,), (1,)), ((), ())), preferred_element_type=F32)
    kr_pad = (_dot_split_lhs(k_rope, place_ref[...], 3) * cos_r
              + _dot_split_lhs(k_rope, placer_ref[...], 3) * sin_r)
    for h in range(MLA_HEADS):
        sl = slice(h * HEAD_PAD, (h + 1) * HEAD_PAD)
        q_ref[0, h] = (q_all[:, sl] * q_mul + q_rot[:, sl] * q_mul_rot).astype(q_ref.dtype)
        k_ref[0, h] = (k_all[:, sl] + kr_pad).astype(k_ref.dtype)
    for hp in range(MLA_HEADS // 2):
        v_ref[0, hp] = v_all_t[hp * LANES:(hp + 1) * LANES, :].astype(v_ref.dtype)


def _mla_prep(lat, cos_t, sin_t, qnw, wq, wqr, kvnw, wk, wv, place, placer):
    B, S, _ = lat.shape
    tm = min(TM_PREP, S)
    H = MLA_HEADS
    cst = lambda shape: pl.BlockSpec(shape, lambda b, i: (0,) * len(shape))
    return pl.pallas_call(
        _mla_prep_kernel,
        grid=(B, S // tm),
        in_specs=[
            pl.BlockSpec((1, tm, LAT_PAD), lambda b, i: (b, i, 0)),
            pl.BlockSpec((1, tm, HEAD_PAD), lambda b, i: (b, i, 0)),
            pl.BlockSpec((1, tm, HEAD_PAD), lambda b, i: (b, i, 0)),
            cst((1, Q_LORA)), cst((Q_LORA, H * HEAD_PAD)), cst((Q_LORA, H * HEAD_PAD)),
            cst((1, KV_LORA)), cst((KV_LORA, H * HEAD_PAD)), cst((MLA_DIM, KV_LORA)),
            cst((QK_ROPE, HEAD_PAD)), cst((QK_ROPE, HEAD_PAD)),
        ],
        out_specs=[
            pl.BlockSpec((1, H, tm, HEAD_PAD), lambda b, i: (b, 0, i, 0)),
            pl.BlockSpec((1, H, tm, HEAD_PAD), lambda b, i: (b, 0, i, 0)),
            pl.BlockSpec((1, H // 2, LANES, tm), lambda b, i: (b, 0, 0, i)),
        ],
        out_shape=[
            jax.ShapeDtypeStruct((B, H, S, HEAD_PAD), BF16),
            jax.ShapeDtypeStruct((B, H, S, HEAD_PAD), BF16),
            jax.ShapeDtypeStruct((B, H // 2, LANES, S), BF16),
        ],
        compiler_params=_cparams("parallel", "parallel"),
        name="mla_prep",
    )(lat, cos_t, sin_t, qnw, wq, wqr, kvnw, wk, wv, place, placer)


def _attn_kernel(q_ref, k_ref, vt_ref, o_ref, *, tq):
    i = pl.program_id(2)
    ki = lax.broadcasted_iota(jnp.int32, (tq, tq), 0)
    qi = lax.broadcasted_iota(jnp.int32, (tq, tq), 1)
    row_v = lax.broadcasted_iota(jnp.int32, (LANES, tq), 0)
    one = jnp.ones((), BF16)
    qs = [q_ref[0, hh] for hh in range(2)]

    def step(j, carry, masked):
        start = pl.multiple_of(j * tq, tq)
        vt = vt_ref[0, 0, :, pl.ds(start, tq)]
        v_aug = [jnp.where(row_v < V_HEAD, vt, one), jnp.where(row_v < V_HEAD, one, vt)]
        ss = [lax.dot_general(k_ref[0, hh, pl.ds(start, tq), :], qs[hh], (((1,), (1,)), ((), ())),
                              preferred_element_type=F32) for hh in range(2)]
        new = []
        ps = []
        for hh in range(2):
            m, acc = carry[hh]
            s = ss[hh]
            if masked:
                s = jnp.where(ki <= qi, s, NEG_BIG)
            m_new = jnp.maximum(m, jnp.max(s, 0, keepdims=True))
            ps.append(jnp.exp(s - m_new).astype(BF16))
            new.append((m_new, jnp.exp(m - m_new) * acc))
        pvs = [jnp.dot(v_aug[hh], ps[hh], preferred_element_type=F32) for hh in range(2)]
        return tuple((new[hh][0], new[hh][1] + pvs[hh]) for hh in range(2))

    init1 = (jnp.full((1, tq), NEG_BIG, F32), jnp.zeros((LANES, tq), F32))
    carry = lax.fori_loop(0, i, functools.partial(step, masked=False), (init1, init1))
    carry = step(i, carry, True)
    a0 = carry[0][1]
    a1 = carry[1][1]
    out_t = jnp.concatenate([a0[0:V_HEAD] / a0[V_HEAD:], a1[V_HEAD:] / a1[0:V_HEAD]], axis=0)
    o_ref[0] = out_t.T.astype(o_ref.dtype)


def _attn(q, k, vt):
    B, H, S, _ = q.shape
    tq = min(TQ, S)
    return pl.pallas_call(
        functools.partial(_attn_kernel, tq=tq),
        grid=(B, H // 2, S // tq),
        in_specs=[
            pl.BlockSpec((1, 2, tq, HEAD_PAD), lambda b, p, i: (b, p, i, 0)),
            pl.BlockSpec((1, 2, S, HEAD_PAD), lambda b, p, i: (b, p, 0, 0)),
            pl.BlockSpec((1, 1, LANES, S), lambda b, p, i: (b, p, 0, 0)),
        ],
        out_specs=pl.BlockSpec((1, tq, LANES), lambda b, p, i: (b, i, p)),
        out_shape=jax.ShapeDtypeStruct((B, S, MLA_DIM), BF16),
        compiler_params=_cparams("parallel", "parallel", "arbitrary"),
        name="attn",
    )(q, k, vt)


def _residual_epilogue(x, f, gate, ln_w, ln_b):
    return _layer_norm(ALPHA * x + gate * f) * ln_w + ln_b


def _merge_out_kernel(x_ref, ya_ref, yb_ref, sga_ref, sgb_ref, pa_ref, pb_ref, wo_ref, mod_ref,
                      lnw_ref, lnb_ref, x1_ref, h_ref):
    m = mod_ref[0]
    tm = x_ref.shape[1]
    halves = [slice(0, tm // 2), slice(tm // 2, tm)]
    mas = [jnp.dot(ya_ref[0, r, :], pa_ref[...], preferred_element_type=F32) for r in halves]
    mbs = [jnp.dot(yb_ref[0, r, :], pb_ref[...], preferred_element_type=F32) for r in halves]
    merged = [(sga_ref[0, r, :].astype(F32) * ma + sgb_ref[0, r, :].astype(F32) * mb).astype(BF16)
              for r, ma, mb in zip(halves, mas, mbs)]
    mixes = [jnp.dot(mg, wo_ref[...], preferred_element_type=F32) for mg in merged]
    for r, mix in zip(halves, mixes):
        x1 = _residual_epilogue(x_ref[0, r, :], mix, m[2:3], lnw_ref[...], lnb_ref[...])
        x1_ref[0, r, :] = x1
        h_ref[0, r, :] = (_layer_norm(x1) * (1.0 + m[4:5]) + m[3:4]).astype(h_ref.dtype)


def _merge_out(x, ya, yb, sg, pa, pb, wo, mod, ln_w, ln_b, h_dtype):
    B, S, _ = x.shape
    tm = min(TM_MERGE, S)
    cst = lambda shape: pl.BlockSpec(shape, lambda b, i: (0,) * len(shape))
    rows = lambda n: pl.BlockSpec((1, tm, n), lambda b, i: (b, i, 0))
    return pl.pallas_call(
        _merge_out_kernel,
        grid=(B, S // tm),
        in_specs=[
            rows(D_MODEL), rows(RW_DIM), rows(MLA_DIM),
            pl.BlockSpec((1, tm, D_MODEL), lambda b, i: (b, i, 0)),
            pl.BlockSpec((1, tm, D_MODEL), lambda b, i: (b, i, 1)),
            cst((RW_DIM, D_MODEL)), cst((MLA_DIM, D_MODEL)), cst((D_MODEL, D_MODEL)),
            pl.BlockSpec((1, 6, D_MODEL), lambda b, i: (b, 0, 0)),
            cst((1, D_MODEL)), cst((1, D_MODEL)),
        ],
        out_specs=[rows(D_MODEL), rows(D_MODEL)],
        out_shape=[
            jax.ShapeDtypeStruct((B, S, D_MODEL), F32),
            jax.ShapeDtypeStruct((B, S, D_MODEL), h_dtype),
        ],
        compiler_params=_cparams("parallel", "parallel"),
        name="merge_out",
    )(x, ya, yb, sg, sg, pa, pb, wo, mod, ln_w, ln_b)


def _write_epilogue(x1, f, m, modn_ref, lnw_ref, lnb_ref, x2_ref, u_ref):
    x2 = _residual_epilogue(x1, f, m[5:6], lnw_ref[...], lnb_ref[...])
    x2_ref[0] = x2
    mn = modn_ref[0]
    u_ref[0] = (_layer_norm(x2) * (1.0 + mn[1:2]) + mn[0:1]).astype(u_ref.dtype)


def _ffn_kernel(x1_ref, h_ref, wg_ref, wu_ref, wd_ref, mod_ref, modn_ref, lnw_ref, lnb_ref,
                x2_ref, u_ref):
    h = h_ref[0]
    acc = None
    for j in range(D_FF // FF_CHUNK):
        sl = slice(j * FF_CHUNK, (j + 1) * FF_CHUNK)
        gt = jnp.dot(h, wg_ref[:, sl], preferred_element_type=F32)
        up = jnp.dot(h, wu_ref[:, sl], preferred_element_type=F32)
        act = (gt * jax.nn.sigmoid(gt) * up).astype(BF16)
        t = jnp.dot(act, wd_ref[sl, :], preferred_element_type=F32)
        acc = t if acc is None else acc + t
    _write_epilogue(x1_ref[0], acc, mod_ref[0], modn_ref, lnw_ref, lnb_ref, x2_ref, u_ref)


def _ffn(x1, h, wg, wu, wd, mod, modn, ln_w, ln_b):
    B, S, _ = x1.shape
    tm = min(TM_FFN, S)
    res = lambda shape: pl.BlockSpec(shape, lambda b, i: (0,) * len(shape), pipeline_mode=pl.Buffered(1))
    cst = lambda shape: pl.BlockSpec(shape, lambda b, i: (0,) * len(shape))
    rows = pl.BlockSpec((1, tm, D_MODEL), lambda b, i: (b, i, 0))
    modspec = pl.BlockSpec((1, 6, D_MODEL), lambda b, i: (b, 0, 0))
    return pl.pallas_call(
        _ffn_kernel,
        grid=(B, S // tm),
        in_specs=[rows, rows, res((D_MODEL, D_FF)), res((D_MODEL, D_FF)), res((D_FF, D_MODEL)),
                  modspec, modspec, cst((1, D_MODEL)), cst((1, D_MODEL))],
        out_specs=[rows, rows],
        out_shape=[
            jax.ShapeDtypeStruct((B, S, D_MODEL), F32),
            jax.ShapeDtypeStruct((B, S, D_MODEL), BF16),
        ],
        compiler_params=_cparams("parallel", "parallel"),
        name="ffn",
    )(x1, h, wg, wu, wd, mod, modn, ln_w, ln_b)


def _router_kernel(h_ref, rw_ref, rb_ref, ltri_ref, rec_ref, cnt_ref, carry_ref):
    first = (pl.program_id(0) == 0) & (pl.program_id(1) == 0)

    @pl.when(first)
    def _init():
        carry_ref[...] = jnp.zeros_like(carry_ref)

    h = h_ref[0]
    tm = h.shape[0]
    logits = jnp.dot(h, rw_ref[...], preferred_element_type=F32, precision=_HI) + rb_ref[...]
    lane = lax.broadcasted_iota(jnp.int32, (tm, LANES), 1)
    m1 = jnp.max(logits, -1, keepdims=True)
    i1 = jnp.min(jnp.where(logits == m1, lane, LANES), -1, keepdims=True)
    rest = jnp.where(lane == i1, -jnp.inf, logits)
    m2 = jnp.max(rest, -1, keepdims=True)
    i2 = jnp.min(jnp.where(rest == m2, lane, LANES), -1, keepdims=True)
    e21 = jnp.exp(m2 - m1)
    g1 = 1.0 / (1.0 + e21)
    g2 = e21 / (1.0 + e21)
    oh1 = lane == i1
    oh2 = lane == i2
    cnt = oh1.astype(F32) + oh2.astype(F32)
    before = jnp.dot(ltri_ref[...], cnt.astype(BF16), preferred_element_type=F32) + carry_ref[0:1, :]
    rank1 = jnp.sum(jnp.where(oh1, before, 0.0), -1, keepdims=True)
    rank2 = jnp.sum(jnp.where(oh2, before, 0.0), -1, keepdims=True)
    carry_ref[0:1, :] = carry_ref[0:1, :] + jnp.sum(cnt, 0, keepdims=True)
    cols = (i1.astype(F32), i2.astype(F32), g1, g2, rank1, rank2)
    rec = jnp.zeros((tm, LANES), F32)
    for c, val in enumerate(cols):
        rec = jnp.where(lane == c, val, rec)
    rec_ref[0] = rec
    cnt_ref[...] = carry_ref[...]


def _router(h, rw_pad, rb_pad, ltri):
    B, S, _ = h.shape
    tm = min(TM_ROUTE, S)
    return pl.pallas_call(
        _router_kernel,
        grid=(B, S // tm),
        in_specs=[
            pl.BlockSpec((1, tm, D_MODEL), lambda b, i: (b, i, 0)),
            pl.BlockSpec((D_MODEL, LANES), lambda b, i: (0, 0)),
            pl.BlockSpec((1, LANES), lambda b, i: (0, 0)),
            pl.BlockSpec((tm, tm), lambda b, i: (0, 0)),
        ],
        out_specs=[
            pl.BlockSpec((1, tm, LANES), lambda b, i: (b, i, 0)),
            pl.BlockSpec((8, LANES), lambda b, i: (0, 0)),
        ],
        out_shape=[
            jax.ShapeDtypeStruct((B, S, LANES), F32),
            jax.ShapeDtypeStruct((8, LANES), F32),
        ],
        scratch_shapes=[pltpu.VMEM((8, LANES), F32)],
        compiler_params=_cparams("arbitrary", "arbitrary"),
        name="router",
    )(h, rw_pad, rb_pad, ltri)


def _expert_kernel(be_ref, nu_ref, tok_ref, h_hbm, wg_ref, wu_ref, wd_ref, ys_ref,
                   rows_ref, x_ref, sems, *, tm):
    i = pl.program_id(0)
    j = pl.program_id(1)
    nb = pl.num_programs(0)
    nj = pl.num_programs(1)
    per_step = tm // EXP_NJ
    used = i < nu_ref[0]

    def row(block, slot, r):
        return pltpu.make_async_copy(h_hbm.at[pl.ds(tok_ref[block * tm + r], 1)],
                                     rows_ref.at[slot, pl.ds(r, 1)], sems.at[slot])

    def wait_block(slot):
        def wait(r, _):
            row(0, slot, r).wait()
            return 0
        lax.fori_loop(0, tm, wait, 0, unroll=8)

    @pl.when((i == 0) & (j == 0))
    def _first():
        def start(r, _):
            row(0, 0, r).start()
            return 0
        lax.fori_loop(0, tm, start, 0, unroll=8)

    @pl.when(j == 0)
    def _arrive():
        wait_block(i % 2)
        x_ref[...] = rows_ref[i % 2].astype(BF16)

    nxt = jnp.minimum(i + 1, nb - 1)
    for r in range(per_step):
        row(nxt, (i + 1) % 2, j * per_step + r).start()

    x = x_ref[...]
    acc = None
    for c in range(wg_ref.shape[2] // FF_CHUNK):
        cols = slice(c * FF_CHUNK, (c + 1) * FF_CHUNK)
        gt = jnp.dot(x, wg_ref[0, :, cols], preferred_element_type=F32)
        up = jnp.dot(x, wu_ref[0, :, cols], preferred_element_type=F32)
        act = (gt * jax.nn.sigmoid(gt) * up).astype(BF16)
        t = jnp.dot(act, wd_ref[0, cols, :], preferred_element_type=F32)
        acc = t if acc is None else acc + t
    acc = jnp.where(used, acc, 0.0)

    @pl.when(j == 0)
    def _():
        ys_ref[...] = acc

    @pl.when(j > 0)
    def _():
        ys_ref[...] += acc

    @pl.when((i == nb - 1) & (j == nj - 1))
    def _drain():
        wait_block((i + 1) % 2)


def _experts(block_e, n_used, slot_tok, h2d, wg, wu, wd):
    m_pad = slot_tok.shape[0]
    tm = TM_EXP
    nj = EXP_NJ
    tf = D_FF_EXPERT // nj

    def jj(i, j, nu):
        return jnp.where(i < nu[0], j, nj - 1)

    return pl.pallas_call(
        functools.partial(_expert_kernel, tm=tm),
        grid_spec=pltpu.PrefetchScalarGridSpec(
            num_scalar_prefetch=3,
            grid=(m_pad // tm, nj),
            in_specs=[
                pl.BlockSpec(memory_space=pl.ANY),
                pl.BlockSpec((1, D_MODEL, tf), lambda i, j, be, nu, tok: (be[i], 0, jj(i, j, nu))),
                pl.BlockSpec((1, D_MODEL, tf), lambda i, j, be, nu, tok: (be[i], 0, jj(i, j, nu))),
                pl.BlockSpec((1, tf, D_MODEL), lambda i, j, be, nu, tok: (be[i], jj(i, j, nu), 0)),
            ],
            out_specs=pl.BlockSpec((tm, D_MODEL), lambda i, j, be, nu, tok: (i, 0)),
            scratch_shapes=[pltpu.VMEM((2, tm, D_MODEL), F32), pltpu.VMEM((tm, D_MODEL), BF16),
                            pltpu.SemaphoreType.DMA((2,))],
        ),
        out_shape=jax.ShapeDtypeStruct((m_pad, D_MODEL), F32),
        compiler_params=_cparams("arbitrary", "arbitrary"),
        name="experts",
    )(block_e, n_used, slot_tok, h2d, wg, wu, wd)


def _combine_kernel(dest_ref, x1_ref, gate_ref, ys_hbm, mod_ref, modn_ref, lnw_ref, lnb_ref,
                    x2_ref, u_ref, buf_ref, sems, *, tm):
    n_inner = pl.num_programs(1)
    tile = pl.program_id(0) * n_inner + pl.program_id(1)
    n_tiles = pl.num_programs(0) * n_inner
    sub = buf_ref.shape[3]
    groups = tm // sub
    slot = tile % 2

    def row(tok0, slot_, group, u, k):
        t = tok0 + group * sub + u
        return pltpu.make_async_copy(ys_hbm.at[pl.ds(dest_ref[2 * t + k], 1)],
                                     buf_ref.at[slot_, k, group, pl.ds(u, 1)], sems.at[slot_])

    def wait_tile(slot_):
        def wait(group, _):
            for u in range(sub):
                row(0, slot_, 0, 0, 0).wait()
                row(0, slot_, 0, 0, 1).wait()
            return 0
        lax.fori_loop(0, groups, wait, 0)

    @pl.when(tile == 0)
    def _first():
        def start(group, _):
            for u in range(sub):
                row(0, 0, group, u, 0).start()
                row(0, 0, group, u, 1).start()
            return 0
        lax.fori_loop(0, groups, start, 0)

    wait_tile(slot)
    nxt_tok0 = jnp.minimum(tile + 1, n_tiles - 1) * tm
    for group in range(groups):
        for u in range(sub):
            row(nxt_tok0, 1 - slot, group, u, 0).start()
            row(nxt_tok0, 1 - slot, group, u, 1).start()
    gate = gate_ref[0]
    f = (gate[:, 0:1] * buf_ref[slot, 0].reshape(tm, D_MODEL)
         + gate[:, 1:2] * buf_ref[slot, 1].reshape(tm, D_MODEL))
    _write_epilogue(x1_ref[0], f, mod_ref[0], modn_ref, lnw_ref, lnb_ref, x2_ref, u_ref)

    @pl.when(tile == n_tiles - 1)
    def _drain():
        wait_tile(1 - slot)


def _combine(dest_flat, x1, gates, ys, mod, modn, ln_w, ln_b):
    B, S, _ = x1.shape
    tm = min(TM_COMB, S)
    rows = pl.BlockSpec((1, tm, D_MODEL), lambda b, i, d: (b, i, 0))
    modspec = pl.BlockSpec((1, 6, D_MODEL), lambda b, i, d: (b, 0, 0))
    cst = pl.BlockSpec((1, D_MODEL), lambda b, i, d: (0, 0))
    return pl.pallas_call(
        functools.partial(_combine_kernel, tm=tm),
        grid_spec=pltpu.PrefetchScalarGridSpec(
            num_scalar_prefetch=1,
            grid=(B, S // tm),
            in_specs=[rows, pl.BlockSpec((1, tm, TOP_K), lambda b, i, d: (b, i, 0)),
                      pl.BlockSpec(memory_space=pl.ANY), modspec, modspec, cst, cst],
            out_specs=[rows, rows],
            scratch_shapes=[pltpu.VMEM((2, TOP_K, tm // SUBLANES, SUBLANES, D_MODEL), F32),
                            pltpu.SemaphoreType.DMA((2,))],
        ),
        out_shape=[
            jax.ShapeDtypeStruct((B, S, D_MODEL), F32),
            jax.ShapeDtypeStruct((B, S, D_MODEL), BF16),
        ],
        compiler_params=_cparams("arbitrary", "arbitrary"),
        name="combine",
    )(dest_flat, x1, gates, ys, mod, modn, ln_w, ln_b)


def _moe(x1, h, rw_pad, rb_pad, ltri, wg, wu, wd, mod, modn, ln_w, ln_b):
    B, S, _ = x1.shape
    n = B * S
    m = n * TOP_K
    rec, cnt = _router(h, rw_pad, rb_pad, ltri)
    rec = rec.reshape(n, LANES)
    e = rec[:, 0:TOP_K].astype(jnp.int32)
    gates = rec[:, TOP_K:2 * TOP_K]
    rank = rec[:, 2 * TOP_K:3 * TOP_K].astype(jnp.int32)
    counts = cnt[0, :N_EXPERTS].astype(jnp.int32)
    padded = (counts + TM_EXP - 1) // TM_EXP * TM_EXP
    pad_end = jnp.cumsum(padded)
    pad_start = pad_end - padded
    dest = (pad_start[e] + rank).reshape(m)
    n_blocks = m // TM_EXP + N_EXPERTS
    block_e = jnp.minimum(
        jnp.searchsorted(pad_end, jnp.arange(n_blocks, dtype=jnp.int32) * TM_EXP, side="right"),
        N_EXPERTS - 1).astype(jnp.int32)
    n_used = (pad_end[-1:] // TM_EXP).astype(jnp.int32)
    slot_tok = jnp.zeros((n_blocks * TM_EXP,), jnp.int32).at[dest].set(
        jnp.arange(m, dtype=jnp.int32) // TOP_K)
    ys = _experts(block_e, n_used, slot_tok, h.reshape(n, D_MODEL), wg, wu, wd)
    return _combine(dest, x1, gates.reshape(B, S, TOP_K), ys, mod, modn, ln_w, ln_b)


def _mla_weights(w_uq, w_ukv):
    L = w_uq.shape[0]
    H = MLA_HEADS
    half = QK_ROPE // 2
    wq = w_uq.reshape(L, Q_LORA, H, QK_NOPE + QK_ROPE)
    nope, rope = wq[..., :QK_NOPE], wq[..., QK_NOPE:]
    zpad = jnp.zeros((L, Q_LORA, H, HEAD_PAD - QK_NOPE - QK_ROPE), w_uq.dtype)
    wq_pad = jnp.concatenate([nope, rope, zpad], -1).reshape(L, Q_LORA, H * HEAD_PAD)
    rope_rot = jnp.concatenate([-rope[..., half:], rope[..., :half]], -1)
    wq_rot = jnp.concatenate([jnp.zeros_like(nope), rope_rot, zpad], -1).reshape(L, Q_LORA, H * HEAD_PAD)
    wkv = w_ukv.reshape(L, KV_LORA, H, QK_NOPE + V_HEAD)
    k_nope, v = wkv[..., :QK_NOPE], wkv[..., QK_NOPE:]
    wk_pad = jnp.concatenate(
        [k_nope, jnp.zeros((L, KV_LORA, H, HEAD_PAD - QK_NOPE), w_ukv.dtype)], -1
    ).reshape(L, KV_LORA, H * HEAD_PAD)
    wv = jnp.swapaxes(v.reshape(L, KV_LORA, MLA_DIM), 1, 2)
    return wq_pad.astype(BF16), wq_rot.astype(BF16), wk_pad.astype(BF16), wv.astype(BF16)


def _rope_placement():
    half = QK_ROPE // 2
    src = jnp.arange(QK_ROPE)[:, None]
    dst = jnp.arange(HEAD_PAD)[None, :]
    place = (dst == src + QK_NOPE).astype(BF16)
    rot = jnp.where((src < half) & (dst == src + QK_NOPE + half), 1.0,
                    jnp.where((src >= half) & (dst == src + QK_NOPE - half), -1.0, 0.0))
    return place, rot.astype(BF16)


def kernel(x, c, positions, w_ada, b_ada, w_in, shift_mu, decay_w0, decay_up, aaa_a0, aaa_up, gate_up, k_k, k_a, r_k, gn_w, gn_b, q_norm_w, w_uq, kv_norm_w, w_ukv, p_a, p_b, w_o, ln1_w, ln1_b, ln2_w, ln2_b, ffn_w_gate, ffn_w_up, ffn_w_down, router_w, router_b, moe_w_gate, moe_w_up, moe_w_down):
    B, S, _ = x.shape
    n = B * S
    L = DEPTH
    assert S % RW_CHUNK == 0 and S % LANES == 0

    mod = _ada(c, w_ada, b_ada).reshape(L, B, 6, D_MODEL)
    cos_t, sin_t = _rope_tables(positions.astype(F32).reshape(B, S, 1))

    w_shift = w_in[:, :, :SHIFT_DIM].astype(BF16)
    w_lat = jnp.pad(w_in[:, :, SHIFT_DIM:SHIFT_DIM + LAT_DIM],
                    ((0, 0), (0, 0), (0, LAT_PAD - LAT_DIM))).astype(BF16)
    w_gates = w_in[:, :, SHIFT_DIM + LAT_DIM:].astype(BF16)
    wq_pad, wq_rot, wk_pad, wv = _mla_weights(w_uq, w_ukv)
    place, placer = _rope_placement()
    lane_head = jnp.arange(RW_DIM) // RW_HEAD
    hsum = (lane_head[:, None] == lane_head[None, :]).astype(BF16)
    row1 = lambda a, l: a[l].reshape(1, -1)
    tm_r = min(TM_ROUTE, S)
    ltri = (jnp.arange(tm_r)[None, :] < jnp.arange(tm_r)[:, None]).astype(BF16)
    rw_pad = jnp.pad(router_w, ((0, 0), (0, 0), (0, LANES - N_EXPERTS)))
    rb_pad = jnp.pad(router_b, ((0, 0), (0, LANES - N_EXPERTS)), constant_values=NEG_BIG)

    u = _lnmod(x, mod[0])
    for l in range(L):
        u2 = u.reshape(n, D_MODEL)
        ps, lat, sg = _proj(u2, w_shift[l], w_lat[l], w_gates[l])
        ps = ps.reshape(B, S, SHIFT_DIM)
        lat = lat.reshape(B, S, LAT_PAD)
        sg = sg.reshape(B, S, 2 * D_MODEL)
        ya = _rwkv(ps, row1(shift_mu, l), row1(decay_w0, l), decay_up[l].astype(BF16), row1(aaa_a0, l),
                   aaa_up[l].astype(BF16), gate_up[l].astype(BF16), row1(k_k, l), row1(k_a, l),
                   row1(r_k, l), row1(gn_w, l), row1(gn_b, l), hsum)
        q, k, v = _mla_prep(lat, cos_t, sin_t, row1(q_norm_w, l), wq_pad[l], wq_rot[l], row1(kv_norm_w, l),
                            wk_pad[l], wv[l], place, placer)
        yb = _attn(q, k, v)
        moe_layer = l % 2 == 1
        x1, h = _merge_out(x, ya, yb, sg, p_a[l].astype(BF16), p_b[l].astype(BF16), w_o[l].astype(BF16),
                           mod[l], row1(ln1_w, l), row1(ln1_b, l), F32 if moe_layer else BF16)
        modn = mod[min(l + 1, L - 1)]
        if moe_layer:
            x, u = _moe(x1, h, rw_pad[l // 2], rb_pad[l // 2].reshape(1, LANES), ltri,
                        moe_w_gate[l // 2].astype(BF16), moe_w_up[l // 2].astype(BF16),
                        moe_w_down[l // 2].astype(BF16), mod[l], modn, row1(ln2_w, l), row1(ln2_b, l))
        else:
            x, u = _ffn(x1, h, ffn_w_gate[l // 2].astype(BF16), ffn_w_up[l // 2].astype(BF16),
                        ffn_w_down[l // 2].astype(BF16), mod[l], modn, row1(ln2_w, l), row1(ln2_b, l))
    return x
```

```python
import functools
import math

import jax
import jax.numpy as jnp
from jax import lax
from jax.experimental import pallas as pl
from jax.experimental.pallas import tpu as pltpu

F32 = jnp.float32
BF16 = jnp.bfloat16

D_MODEL = 1024
DEPTH = 4
RW_HEADS = 8
RW_HEAD = 64
RW_DIM = RW_HEADS * RW_HEAD
DECAY_LORA = 64
AAA_LORA = 64
GATE_LORA = 128
GN_EPS = 64e-5
MLA_HEADS = 8
QK_NOPE = 64
QK_ROPE = 32
V_HEAD = 64
Q_LORA = 384
KV_LORA = 256
MLA_DIM = MLA_HEADS * V_HEAD
ROPE_THETA = 10000.0
SHIFT_DIM = 3 * RW_DIM + DECAY_LORA + AAA_LORA + GATE_LORA
LAT_DIM = Q_LORA + KV_LORA + QK_ROPE
LAT_PAD = 768
D_FF = 2816
N_EXPERTS = 8
TOP_K = 2
D_FF_EXPERT = 3584
ALPHA = (2.0 * DEPTH) ** 0.25
LN_EPS = 1e-5
RMS_EPS = 1e-6
LANES = 128
SUBLANES = 8
HEAD_PAD = 128
NEG_BIG = -1e30

RW_CHUNK = 64
RW_SEQS = 4
TM_MM = 512
TM_PREP = 512
TQ = 1024
TM_MERGE = 512
TM_FFN = 512
FF_CHUNK = 256
TM_ROUTE = 512
TM_EXP = 512
EXP_NJ = 2
TM_COMB = 256

_HI = lax.Precision.HIGHEST


def _cparams(*sem):
    return pltpu.CompilerParams(dimension_semantics=sem)


def _dot(a, b):
    return jnp.dot(a.astype(BF16), b.astype(BF16), preferred_element_type=F32)


def _dot_nt(a, b):
    return lax.dot_general(a.astype(BF16), b.astype(BF16), (((1,), (1,)), ((), ())),
                           preferred_element_type=F32)


def _dot_tn(a, b):
    return lax.dot_general(a.astype(BF16), b.astype(BF16), (((0,), (0,)), ((), ())),
                           preferred_element_type=F32)


def _split(x, n):
    parts = []
    for _ in range(n - 1):
        p = x.astype(BF16)
        parts.append(p)
        x = x - p.astype(F32)
    parts.append(x.astype(BF16))
    return parts


def _dot_split_lhs(x, w_bf16, n):
    acc = None
    for p in _split(x, n):
        t = jnp.dot(p, w_bf16, preferred_element_type=F32)
        acc = t if acc is None else acc + t
    return acc


def _dot_split_rhs(w_bf16, x, n):
    acc = None
    for p in _split(x, n):
        t = jnp.dot(w_bf16, p, preferred_element_type=F32)
        acc = t if acc is None else acc + t
    return acc


def _layer_norm(x):
    mu = jnp.mean(x, -1, keepdims=True)
    xc = x - mu
    var = jnp.mean(xc * xc, -1, keepdims=True)
    return xc * lax.rsqrt(var + LN_EPS)


def _full(shape):
    nd = len(shape)
    return pl.BlockSpec(shape, lambda *_: (0,) * nd)


def _ada_kernel(c_ref, w_ref, b_ref, o_ref):
    c = c_ref[...]
    ca = c * jax.nn.sigmoid(c)
    o_ref[0] = jnp.dot(ca, w_ref[0], preferred_element_type=F32, precision=_HI) + b_ref[0]


def _ada(c, w_ada, b_ada):
    L, _, six_d = w_ada.shape
    B = c.shape[0]
    nj = six_d // D_MODEL
    return pl.pallas_call(
        _ada_kernel,
        grid=(L, nj),
        in_specs=[
            pl.BlockSpec((B, D_MODEL), lambda l, j: (0, 0)),
            pl.BlockSpec((1, D_MODEL, D_MODEL), lambda l, j: (l, 0, j)),
            pl.BlockSpec((1, 1, D_MODEL), lambda l, j: (l, 0, j)),
        ],
        out_specs=pl.BlockSpec((1, B, D_MODEL), lambda l, j: (l, 0, j)),
        out_shape=jax.ShapeDtypeStruct((L, B, six_d), F32),
        compiler_params=_cparams("parallel", "parallel"),
        name="ada",
    )(c, w_ada, b_ada.reshape(L, 1, six_d))


def _lnmod_kernel(x_ref, mod_ref, o_ref):
    m = mod_ref[0]
    o_ref[0] = (_layer_norm(x_ref[0]) * (1.0 + m[1:2]) + m[0:1]).astype(o_ref.dtype)


def _lnmod(x, mod):
    B, S, _ = x.shape
    tm = min(TM_MM, S)
    return pl.pallas_call(
        _lnmod_kernel,
        grid=(B, S // tm),
        in_specs=[
            pl.BlockSpec((1, tm, D_MODEL), lambda b, i: (b, i, 0)),
            pl.BlockSpec((1, 6, D_MODEL), lambda b, i: (b, 0, 0)),
        ],
        out_specs=pl.BlockSpec((1, tm, D_MODEL), lambda b, i: (b, i, 0)),
        out_shape=jax.ShapeDtypeStruct((B, S, D_MODEL), BF16),
        compiler_params=_cparams("parallel", "parallel"),
        name="lnmod",
    )(x, mod)


def _proj_kernel(u_ref, ws_ref, wl_ref, wg_ref, ps_ref, lat_ref, sg_ref):
    u = u_ref[...]
    ps_ref[...] = jnp.dot(u, ws_ref[...], preferred_element_type=F32)
    lat_ref[...] = jnp.dot(u, wl_ref[...], preferred_element_type=F32)
    sg_ref[...] = jax.nn.sigmoid(jnp.dot(u, wg_ref[...], preferred_element_type=F32)).astype(sg_ref.dtype)


def _proj(u2, w_shift, w_lat, w_gates):
    n = u2.shape[0]
    tm = min(TM_MM, n)
    res = lambda w: pl.BlockSpec(w.shape, lambda i: (0, 0), pipeline_mode=pl.Buffered(1))
    rows = lambda c: pl.BlockSpec((tm, c), lambda i: (i, 0))
    return pl.pallas_call(
        _proj_kernel,
        grid=(n // tm,),
        in_specs=[rows(D_MODEL), res(w_shift), res(w_lat), res(w_gates)],
        out_specs=[rows(SHIFT_DIM), rows(LAT_PAD), rows(2 * D_MODEL)],
        out_shape=[
            jax.ShapeDtypeStruct((n, SHIFT_DIM), F32),
            jax.ShapeDtypeStruct((n, LAT_PAD), F32),
            jax.ShapeDtypeStruct((n, 2 * D_MODEL), BF16),
        ],
        compiler_params=_cparams("parallel"),
        name="proj",
    )(u2, w_shift, w_lat, w_gates)


def _rwkv_kernel(ps_ref, mu_ref, w0_ref, dup_ref, a0_ref, aup_ref, gup_ref, kk_ref, ka_ref, rk_ref,
                 gnw_ref, gnb_ref, hsum_ref, o_ref, state_ref, prev_ref, y_ref):
    C = RW_CHUNK
    N = RW_HEAD
    G = ps_ref.shape[0]

    @pl.when(pl.program_id(1) == 0)
    def _init():
        state_ref[...] = jnp.zeros_like(state_ref)
        prev_ref[...] = jnp.zeros_like(prev_ref)

    half = RW_DIM // 2
    hs_lo = hsum_ref[0:half, 0:half]
    hs_hi = hsum_ref[half:, half:]

    def head_sum(z):
        zb = z.astype(BF16)
        return jnp.concatenate([jnp.dot(zb[:, 0:half], hs_lo, preferred_element_type=F32),
                                jnp.dot(zb[:, half:], hs_hi, preferred_element_type=F32)], axis=1)

    ri = lax.broadcasted_iota(jnp.int32, (C, C), 0)
    ci = lax.broadcasted_iota(jnp.int32, (C, C), 1)
    tri = (ci <= ri).astype(BF16)
    r4 = lax.broadcasted_iota(jnp.int32, (2 * C, 4 * C), 0)
    c4 = lax.broadcasted_iota(jnp.int32, (2 * C, 4 * C), 1)
    keep = (c4 % C) < (r4 % C) + (r4 >= C).astype(jnp.int32)
    r1 = lax.broadcasted_iota(jnp.int32, (C, 2 * C), 0)
    c1 = lax.broadcasted_iota(jnp.int32, (C, 2 * C), 1)
    low = c1 < C
    eye_hi = (c1 == r1 + C).astype(F32)

    prep = []
    for s in range(G):
        p = ps_ref[s]
        row = lax.broadcasted_iota(jnp.int32, p.shape, 0)
        shifted = jnp.where(row == 0, prev_ref[s, 0:1, :], pltpu.roll(p, 1, 0))
        prev_ref[s, 0:1, :] = p[C - 1:C, :]
        xs = p + (shifted - p) * mu_ref[...]
        r = xs[:, 0:RW_DIM]
        k = xs[:, RW_DIM:2 * RW_DIM]
        v = xs[:, 2 * RW_DIM:3 * RW_DIM]
        o = 3 * RW_DIM
        dw = xs[:, o:o + DECAY_LORA]
        da = xs[:, o + DECAY_LORA:o + DECAY_LORA + AAA_LORA]
        dg = xs[:, o + DECAY_LORA + AAA_LORA:SHIFT_DIM]
        lw = -math.exp(-0.5) * jax.nn.sigmoid(w0_ref[...] + _dot(jnp.tanh(dw), dup_ref[...]))
        a = jax.nn.sigmoid(a0_ref[...] + _dot(da, aup_ref[...]))
        gate = _dot(jax.nn.sigmoid(dg), gup_ref[...])
        kk = k * kk_ref[...]
        kk = kk * lax.rsqrt(jnp.maximum(head_sum(kk * kk), 1e-24))
        k_mod = k * (1.0 + (a - 1.0) * ka_ref[...])
        b = kk * a
        bonus = head_sum(r * k_mod * rk_ref[...]) * v
        cw = _dot_split_rhs(tri, lw, 2)
        cw_last = cw[C - 1:C, :]
        e_neg = jnp.exp(-cw)
        e_rem = jnp.exp(cw_last - cw)
        prep.append(dict(
            kp=(kk * jnp.exp(cw - lw)).astype(BF16), rp=(r * jnp.exp(cw)).astype(BF16),
            bm=(b * e_neg).astype(BF16), km=(k_mod * e_neg).astype(BF16),
            bt=(b * e_rem).astype(BF16), kt=(k_mod * e_rem).astype(BF16),
            v=v.astype(BF16), w_c=jnp.exp(cw_last), bonus=bonus, gate=gate))

    chains = [(s, h) for s in range(G) for h in range(RW_HEADS)]
    sl = lambda h: slice(h * N, (h + 1) * N)

    krs, aas = [], []
    for s, h in chains:
        d = prep[s]
        kr = jnp.concatenate([d["kp"][:, sl(h)], d["rp"][:, sl(h)]], axis=0)
        bk = jnp.concatenate([d["bm"][:, sl(h)], d["km"][:, sl(h)], d["km"][:, sl(h)], d["bm"][:, sl(h)]],
                             axis=0)
        krs.append(kr)
        aas.append(jnp.where(keep, _dot_nt(kr, bk), 0.0))
    s0s = [state_ref[s, h] for s, h in chains]
    xxs = [_dot_nt(kr, s0) for kr, s0 in zip(krs, s0s)]
    rts = [jnp.where(low, _dot(aa[0:C, 0:C], aa[0:C, 0:2 * C]), eye_hi - aa[0:C, 2 * C:4 * C]) for aa in aas]
    for _ in range(int(math.log2(C)) - 1):
        outs = [_dot(rt[:, 0:C], rt) for rt in rts]
        rts = [jnp.where(low, out, rt + out) for rt, out in zip(rts, outs)]
    vs = [prep[s]["v"][:, sl(h)] for s, h in chains]
    gms = [xx[0:C] + _dot(aa[0:C, 2 * C:3 * C], v_h) for xx, aa, v_h in zip(xxs, aas, vs)]
    us = [-_dot(rt[:, C:2 * C], gm) for rt, gm in zip(rts, gms)]
    uvs = [jnp.concatenate([u.astype(BF16), v_h], axis=0) for u, v_h in zip(us, vs)]
    ys = [xx[C:2 * C] + _dot(aa[C:2 * C, 0:2 * C], uv) for xx, aa, uv in zip(xxs, aas, uvs)]
    news = []
    for (s, h), s0, uv in zip(chains, s0s, uvs):
        d = prep[s]
        btkt = jnp.concatenate([d["bt"][:, sl(h)], d["kt"][:, sl(h)]], axis=0)
        news.append(s0 * d["w_c"][:, sl(h)] + _dot_tn(uv, btkt))
    for (s, h), y_h, new in zip(chains, ys, news):
        y_ref[s, :, sl(h)] = y_h
        state_ref[s, h] = new

    for s in range(G):
        d = prep[s]
        y = y_ref[s]
        mean = head_sum(y) * (1.0 / N)
        yc = y - mean
        var = head_sum(yc * yc) * (1.0 / N)
        yn = yc * lax.rsqrt(var + GN_EPS) * gnw_ref[...] + gnb_ref[...]
        o_ref[s] = ((yn + d["bonus"]) * d["gate"]).astype(o_ref.dtype)


def _rwkv(ps, mu, w0, dup, a0, aup, gup, k_k, k_a, r_k, gn_w, gn_b, hsum):
    B, S, _ = ps.shape
    C = RW_CHUNK
    G = RW_SEQS if B % RW_SEQS == 0 else 1
    row = lambda n: pl.BlockSpec((1, n), lambda b, j: (0, 0))
    return pl.pallas_call(
        _rwkv_kernel,
        grid=(B // G, S // C),
        in_specs=[
            pl.BlockSpec((G, C, SHIFT_DIM), lambda b, j: (b, j, 0)),
            row(SHIFT_DIM), row(RW_DIM),
            pl.BlockSpec((DECAY_LORA, RW_DIM), lambda b, j: (0, 0)),
            row(RW_DIM),
            pl.BlockSpec((AAA_LORA, RW_DIM), lambda b, j: (0, 0)),
            pl.BlockSpec((GATE_LORA, RW_DIM), lambda b, j: (0, 0)),
            row(RW_DIM), row(RW_DIM), row(RW_DIM), row(RW_DIM), row(RW_DIM),
            pl.BlockSpec((RW_DIM, RW_DIM), lambda b, j: (0, 0)),
        ],
        out_specs=pl.BlockSpec((G, C, RW_DIM), lambda b, j: (b, j, 0)),
        out_shape=jax.ShapeDtypeStruct((B, S, RW_DIM), BF16),
        scratch_shapes=[
            pltpu.VMEM((G, RW_HEADS, RW_HEAD, RW_HEAD), F32),
            pltpu.VMEM((G, 8, SHIFT_DIM), F32),
            pltpu.VMEM((G, C, RW_DIM), F32),
        ],
        compiler_params=_cparams("parallel", "arbitrary"),
        name="rwkv",
    )(ps, mu, w0, dup, a0, aup, gup, k_k, k_a, r_k, gn_w, gn_b, hsum)


def _rope_kernel(pos_ref, cos_ref, sin_ref):
    half = QK_ROPE // 2
    lane = lax.broadcasted_iota(jnp.int32, (1, HEAD_PAD), 1)
    in_rope = (lane >= QK_NOPE) & (lane < QK_NOPE + QK_ROPE)
    fidx = jnp.where(lane < QK_NOPE + half, lane - QK_NOPE, lane - QK_NOPE - half)
    inv_freq = jnp.exp(-math.log(ROPE_THETA) * fidx.astype(F32) / half)
    ang = pos_ref[0] * inv_freq
    cos_ref[0] = jnp.where(in_rope, jnp.cos(ang), 0.0)
    sin_ref[0] = jnp.where(in_rope, jnp.sin(ang), 0.0)


def _rope_tables(posf):
    B, S, _ = posf.shape
    tm = min(TM_PREP, S)
    out = pl.BlockSpec((1, tm, HEAD_PAD), lambda b, i: (b, i, 0))
    return pl.pallas_call(
        _rope_kernel,
        grid=(B, S // tm),
        in_specs=[pl.BlockSpec((1, tm, 1), lambda b, i: (b, i, 0))],
        out_specs=[out, out],
        out_shape=[jax.ShapeDtypeStruct((B, S, HEAD_PAD), F32)] * 2,
        compiler_params=_cparams("parallel", "parallel"),
        name="rope_tables",
    )(posf)


def _mla_prep_kernel(lat_ref, cos_ref, sin_ref, qnw_ref, wq_ref, wqr_ref, kvnw_ref, wk_ref, wv_ref,
                     place_ref, placer_ref, q_ref, k_ref, v_ref):
    lat = lat_ref[0]
    c_q = lat[:, 0:Q_LORA]
    c_kv = lat[:, Q_LORA:Q_LORA + KV_LORA]
    k_rope = lat[:, Q_LORA + KV_LORA:LAT_DIM]

    def rms(z, w):
        return z * lax.rsqrt(jnp.mean(z * z, -1, keepdims=True) + RMS_EPS) * w

    lane = lax.broadcasted_iota(jnp.int32, (1, HEAD_PAD), 1)
    cos_r = cos_ref[0]
    sin_r = sin_ref[0]
    scale = (QK_NOPE + QK_ROPE) ** -0.5
    q_mul = jnp.where(lane < QK_NOPE, 1.0, cos_r) * scale
    q_mul_rot = sin_r * scale

    cq_n = rms(c_q, qnw_ref[...]).astype(BF16)
    q_all = jnp.dot(cq_n, wq_ref[...], preferred_element_type=F32)
    q_rot = jnp.dot(cq_n, wqr_ref[...], preferred_element_type=F32)
    ckv_n = rms(c_kv, kvnw_ref[...]).astype(BF16)
    k_all = jnp.dot(ckv_n, wk_ref[...], preferred_element_type=F32)
    v_all_t = lax.dot_general(wv_ref[...], ckv_n, (((1,), (1,)), ((), ())), preferred_element_type=F32)
    kr_pad = (_dot_split_lhs(k_rope, place_ref[...], 3) * cos_r
              + _dot_split_lhs(k_rope, placer_ref[...], 3) * sin_r)
    for h in range(MLA_HEADS):
        sl = slice(h * HEAD_PAD, (h + 1) * HEAD_PAD)
        q_ref[0, h] = (q_all[:, sl] * q_mul + q_rot[:, sl] * q_mul_rot).astype(q_ref.dtype)
        k_ref[0, h] = (k_all[:, sl] + kr_pad).astype(k_ref.dtype)
    for hp in range(MLA_HEADS // 2):
        v_ref[0, hp] = v_all_t[hp * LANES:(hp + 1) * LANES, :].astype(v_ref.dtype)


def _mla_prep(lat, cos_t, sin_t, qnw, wq, wqr, kvnw, wk, wv, place, placer):
    B, S, _ = lat.shape
    tm = min(TM_PREP, S)
    H = MLA_HEADS
    cst = lambda shape: pl.BlockSpec(shape, lambda b, i: (0,) * len(shape))
    return pl.pallas_call(
        _mla_prep_kernel,
        grid=(B, S // tm),
        in_specs=[
            pl.BlockSpec((1, tm, LAT_PAD), lambda b, i: (b, i, 0)),
            pl.BlockSpec((1, tm, HEAD_PAD), lambda b, i: (b, i, 0)),
            pl.BlockSpec((1, tm, HEAD_PAD), lambda b, i: (b, i, 0)),
            cst((1, Q_LORA)), cst((Q_LORA, H * HEAD_PAD)), cst((Q_LORA, H * HEAD_PAD)),
            cst((1, KV_LORA)), cst((KV_LORA, H * HEAD_PAD)), cst((MLA_DIM, KV_LORA)),
            cst((QK_ROPE, HEAD_PAD)), cst((QK_ROPE, HEAD_PAD)),
        ],
        out_specs=[
            pl.BlockSpec((1, H, tm, HEAD_PAD), lambda b, i: (b, 0, i, 0)),
            pl.BlockSpec((1, H, tm, HEAD_PAD), lambda b, i: (b, 0, i, 0)),
            pl.BlockSpec((1, H // 2, LANES, tm), lambda b, i: (b, 0, 0, i)),
        ],
        out_shape=[
            jax.ShapeDtypeStruct((B, H, S, HEAD_PAD), BF16),
            jax.ShapeDtypeStruct((B, H, S, HEAD_PAD), BF16),
            jax.ShapeDtypeStruct((B, H // 2, LANES, S), BF16),
        ],
        compiler_params=_cparams("parallel", "parallel"),
        name="mla_prep",
    )(lat, cos_t, sin_t, qnw, wq, wqr, kvnw, wk, wv, place, placer)


def _attn_kernel(q_ref, k_ref, vt_ref, o_ref, *, tq):
    i = pl.program_id(2)
    ki = lax.broadcasted_iota(jnp.int32, (tq, tq), 0)
    qi = lax.broadcasted_iota(jnp.int32, (tq, tq), 1)
    row_v = lax.broadcasted_iota(jnp.int32, (LANES, tq), 0)
    one = jnp.ones((), BF16)
    qs = [q_ref[0, hh] for hh in range(2)]

    def step(j, carry, masked):
        start = pl.multiple_of(j * tq, tq)
        vt = vt_ref[0, 0, :, pl.ds(start, tq)]
        v_aug = [jnp.where(row_v < V_HEAD, vt, one), jnp.where(row_v < V_HEAD, one, vt)]
        ss = [lax.dot_general(k_ref[0, hh, pl.ds(start, tq), :], qs[hh], (((1,), (1,)), ((), ())),
                              preferred_element_type=F32) for hh in range(2)]
        new = []
        ps = []
        for hh in range(2):
            m, acc = carry[hh]
            s = ss[hh]
            if masked:
                s = jnp.where(ki <= qi, s, NEG_BIG)
            m_new = jnp.maximum(m, jnp.max(s, 0, keepdims=True))
            ps.append(jnp.exp(s - m_new).astype(BF16))
            new.append((m_new, jnp.exp(m - m_new) * acc))
        pvs = [jnp.dot(v_aug[hh], ps[hh], preferred_element_type=F32) for hh in range(2)]
        return tuple((new[hh][0], new[hh][1] + pvs[hh]) for hh in range(2))

    init1 = (jnp.full((1, tq), NEG_BIG, F32), jnp.zeros((LANES, tq), F32))
    carry = lax.fori_loop(0, i, functools.partial(step, masked=False), (init1, init1))
    carry = step(i, carry, True)
    a0 = carry[0][1]
    a1 = carry[1][1]
    out_t = jnp.concatenate([a0[0:V_HEAD] / a0[V_HEAD:], a1[V_HEAD:] / a1[0:V_HEAD]], axis=0)
    o_ref[0] = out_t.T.astype(o_ref.dtype)


def _attn(q, k, vt):
    B, H, S, _ = q.shape
    tq = min(TQ, S)
    return pl.pallas_call(
        functools.partial(_attn_kernel, tq=tq),
        grid=(B, H // 2, S // tq),
        in_specs=[
            pl.BlockSpec((1, 2, tq, HEAD_PAD), lambda b, p, i: (b, p, i, 0)),
            pl.BlockSpec((1, 2, S, HEAD_PAD), lambda b, p, i: (b, p, 0, 0)),
            pl.BlockSpec((1, 1, LANES, S), lambda b, p, i: (b, p, 0, 0)),
        ],
        out_specs=pl.BlockSpec((1, tq, LANES), lambda b, p, i: (b, i, p)),
        out_shape=jax.ShapeDtypeStruct((B, S, MLA_DIM), BF16),
        compiler_params=_cparams("parallel", "parallel", "arbitrary"),
        name="attn",
    )(q, k, vt)


def _residual_epilogue(x, f, gate, ln_w, ln_b):
    return _layer_norm(ALPHA * x + gate * f) * ln_w + ln_b


def _merge_out_kernel(x_ref, ya_ref, yb_ref, sga_ref, sgb_ref, pa_ref, pb_ref, wo_ref, mod_ref,
                      lnw_ref, lnb_ref, x1_ref, h_ref):
    m = mod_ref[0]
    tm = x_ref.shape[1]
    halves = [slice(0, tm // 2), slice(tm // 2, tm)]
    mas = [jnp.dot(ya_ref[0, r, :], pa_ref[...], preferred_element_type=F32) for r in halves]
    mbs = [jnp.dot(yb_ref[0, r, :], pb_ref[...], preferred_element_type=F32) for r in halves]
    merged = [(sga_ref[0, r, :].astype(F32) * ma + sgb_ref[0, r, :].astype(F32) * mb).astype(BF16)
              for r, ma, mb in zip(halves, mas, mbs)]
    mixes = [jnp.dot(mg, wo_ref[...], preferred_element_type=F32) for mg in merged]
    for r, mix in zip(halves, mixes):
        x1 = _residual_epilogue(x_ref[0, r, :], mix, m[2:3], lnw_ref[...], lnb_ref[...])
        x1_ref[0, r, :] = x1
        h_ref[0, r, :] = (_layer_norm(x1) * (1.0 + m[4:5]) + m[3:4]).astype(h_ref.dtype)


def _merge_out(x, ya, yb, sg, pa, pb, wo, mod, ln_w, ln_b, h_dtype):
    B, S, _ = x.shape
    tm = min(TM_MERGE, S)
    cst = lambda shape: pl.BlockSpec(shape, lambda b, i: (0,) * len(shape))
    rows = lambda n: pl.BlockSpec((1, tm, n), lambda b, i: (b, i, 0))
    return pl.pallas_call(
        _merge_out_kernel,
        grid=(B, S // tm),
        in_specs=[
            rows(D_MODEL), rows(RW_DIM), rows(MLA_DIM),
            pl.BlockSpec((1, tm, D_MODEL), lambda b, i: (b, i, 0)),
            pl.BlockSpec((1, tm, D_MODEL), lambda b, i: (b, i, 1)),
            cst((RW_DIM, D_MODEL)), cst((MLA_DIM, D_MODEL)), cst((D_MODEL, D_MODEL)),
            pl.BlockSpec((1, 6, D_MODEL), lambda b, i: (b, 0, 0)),
            cst((1, D_MODEL)), cst((1, D_MODEL)),
        ],
        out_specs=[rows(D_MODEL), rows(D_MODEL)],
        out_shape=[
            jax.ShapeDtypeStruct((B, S, D_MODEL), F32),
            jax.ShapeDtypeStruct((B, S, D_MODEL), h_dtype),
        ],
        compiler_params=_cparams("parallel", "parallel"),
        name="merge_out",
    )(x, ya, yb, sg, sg, pa, pb, wo, mod, ln_w, ln_b)


def _write_epilogue(x1, f, m, modn_ref, lnw_ref, lnb_ref, x2_ref, u_ref):
    x2 = _residual_epilogue(x1, f, m[5:6], lnw_ref[...], lnb_ref[...])
    x2_ref[0] = x2
    mn = modn_ref[0]
    u_ref[0] = (_layer_norm(x2) * (1.0 + mn[1:2]) + mn[0:1]).astype(u_ref.dtype)


def _ffn_kernel(x1_ref, h_ref, wg_ref, wu_ref, wd_ref, mod_ref, modn_ref, lnw_ref, lnb_ref,
                x2_ref, u_ref):
    h = h_ref[0]
    acc = None
    for j in range(D_FF // FF_CHUNK):
        sl = slice(j * FF_CHUNK, (j + 1) * FF_CHUNK)
        gt = jnp.dot(h, wg_ref[:, sl], preferred_element_type=F32)
        up = jnp.dot(h, wu_ref[:, sl], preferred_element_type=F32)
        act = (gt * jax.nn.sigmoid(gt) * up).astype(BF16)
        t = jnp.dot(act, wd_ref[sl, :], preferred_element_type=F32)
        acc = t if acc is None else acc + t
    _write_epilogue(x1_ref[0], acc, mod_ref[0], modn_ref, lnw_ref, lnb_ref, x2_ref, u_ref)


def _ffn(x1, h, wg, wu, wd, mod, modn, ln_w, ln_b):
    B, S, _ = x1.shape
    tm = min(TM_FFN, S)
    res = lambda shape: pl.BlockSpec(shape, lambda b, i: (0,) * len(shape), pipeline_mode=pl.Buffered(1))
    cst = lambda shape: pl.BlockSpec(shape, lambda b, i: (0,) * len(shape))
    rows = pl.BlockSpec((1, tm, D_MODEL), lambda b, i: (b, i, 0))
    modspec = pl.BlockSpec((1, 6, D_MODEL), lambda b, i: (b, 0, 0))
    return pl.pallas_call(
        _ffn_kernel,
        grid=(B, S // tm),
        in_specs=[rows, rows, res((D_MODEL, D_FF)), res((D_MODEL, D_FF)), res((D_FF, D_MODEL)),
                  modspec, modspec, cst((1, D_MODEL)), cst((1, D_MODEL))],
        out_specs=[rows, rows],
        out_shape=[
            jax.ShapeDtypeStruct((B, S, D_MODEL), F32),
            jax.ShapeDtypeStruct((B, S, D_MODEL), BF16),
        ],
        compiler_params=_cparams("parallel", "parallel"),
        name="ffn",
    )(x1, h, wg, wu, wd, mod, modn, ln_w, ln_b)


def _router_kernel(h_ref, rw_ref, rb_ref, ltri_ref, rec_ref, cnt_ref, carry_ref):
    first = (pl.program_id(0) == 0) & (pl.program_id(1) == 0)

    @pl.when(first)
    def _init():
        carry_ref[...] = jnp.zeros_like(carry_ref)

    h = h_ref[0]
    tm = h.shape[0]
    logits = jnp.dot(h, rw_ref[...], preferred_element_type=F32, precision=_HI) + rb_ref[...]
    lane = lax.broadcasted_iota(jnp.int32, (tm, LANES), 1)
    m1 = jnp.max(logits, -1, keepdims=True)
    i1 = jnp.min(jnp.where(logits == m1, lane, LANES), -1, keepdims=True)
    rest = jnp.where(lane == i1, -jnp.inf, logits)
    m2 = jnp.max(rest, -1, keepdims=True)
    i2 = jnp.min(jnp.where(rest == m2, lane, LANES), -1, keepdims=True)
    e21 = jnp.exp(m2 - m1)
    g1 = 1.0 / (1.0 + e21)
    g2 = e21 / (1.0 + e21)
    oh1 = lane == i1
    oh2 = lane == i2
    cnt = oh1.astype(F32) + oh2.astype(F32)
    before = jnp.dot(ltri_ref[...], cnt.astype(BF16), preferred_element_type=F32) + carry_ref[0:1, :]
    rank1 = jnp.sum(jnp.where(oh1, before, 0.0), -1, keepdims=True)
    rank2 = jnp.sum(jnp.where(oh2, before, 0.0), -1, keepdims=True)
    carry_ref[0:1, :] = carry_ref[0:1, :] + jnp.sum(cnt, 0, keepdims=True)
    cols = (i1.astype(F32), i2.astype(F32), g1, g2, rank1, rank2)
    rec = jnp.zeros((tm, LANES), F32)
    for c, val in enumerate(cols):
        rec = jnp.where(lane == c, val, rec)
    rec_ref[0] = rec
    cnt_ref[...] = carry_ref[...]


def _router(h, rw_pad, rb_pad, ltri):
    B, S, _ = h.shape
    tm = min(TM_ROUTE, S)
    return pl.pallas_call(
        _router_kernel,
        grid=(B, S // tm),
        in_specs=[
            pl.BlockSpec((1, tm, D_MODEL), lambda b, i: (b, i, 0)),
            pl.BlockSpec((D_MODEL, LANES), lambda b, i: (0, 0)),
            pl.BlockSpec((1, LANES), lambda b, i: (0, 0)),
            pl.BlockSpec((tm, tm), lambda b, i: (0, 0)),
        ],
        out_specs=[
            pl.BlockSpec((1, tm, LANES), lambda b, i: (b, i, 0)),
            pl.BlockSpec((8, LANES), lambda b, i: (0, 0)),
        ],
        out_shape=[
            jax.ShapeDtypeStruct((B, S, LANES), F32),
            jax.ShapeDtypeStruct((8, LANES), F32),
        ],
        scratch_shapes=[pltpu.VMEM((8, LANES), F32)],
        compiler_params=_cparams("arbitrary", "arbitrary"),
        name="router",
    )(h, rw_pad, rb_pad, ltri)


def _expert_kernel(be_ref, nu_ref, tok_ref, h_hbm, wg_ref, wu_ref, wd_ref, ys_ref,
                   rows_ref, x_ref, sems, *, tm):
    i = pl.program_id(0)
    j = pl.program_id(1)
    nb = pl.num_programs(0)
    nj = pl.num_programs(1)
    per_step = tm // EXP_NJ
    used = i < nu_ref[0]

    def row(block, slot, r):
        return pltpu.make_async_copy(h_hbm.at[pl.ds(tok_ref[block * tm + r], 1)],
                                     rows_ref.at[slot, pl.ds(r, 1)], sems.at[slot])

    def wait_block(slot):
        def wait(r, _):
            row(0, slot, r).wait()
            return 0
        lax.fori_loop(0, tm, wait, 0, unroll=8)

    @pl.when((i == 0) & (j == 0))
    def _first():
        def start(r, _):
            row(0, 0, r).start()
            return 0
        lax.fori_loop(0, tm, start, 0, unroll=8)

    @pl.when(j == 0)
    def _arrive():
        wait_block(i % 2)
        x_ref[...] = rows_ref[i % 2].astype(BF16)

    nxt = jnp.minimum(i + 1, nb - 1)
    for r in range(per_step):
        row(nxt, (i + 1) % 2, j * per_step + r).start()

    x = x_ref[...]
    acc = None
    for c in range(wg_ref.shape[2] // FF_CHUNK):
        cols = slice(c * FF_CHUNK, (c + 1) * FF_CHUNK)
        gt = jnp.dot(x, wg_ref[0, :, cols], preferred_element_type=F32)
        up = jnp.dot(x, wu_ref[0, :, cols], preferred_element_type=F32)
        act = (gt * jax.nn.sigmoid(gt) * up).astype(BF16)
        t = jnp.dot(act, wd_ref[0, cols, :], preferred_element_type=F32)
        acc = t if acc is None else acc + t
    acc = jnp.where(used, acc, 0.0)

    @pl.when(j == 0)
    def _():
        ys_ref[...] = acc

    @pl.when(j > 0)
    def _():
        ys_ref[...] += acc

    @pl.when((i == nb - 1) & (j == nj - 1))
    def _drain():
        wait_block((i + 1) % 2)


def _experts(block_e, n_used, slot_tok, h2d, wg, wu, wd):
    m_pad = slot_tok.shape[0]
    tm = TM_EXP
    nj = EXP_NJ
    tf = D_FF_EXPERT // nj

    def jj(i, j, nu):
        return jnp.where(i < nu[0], j, nj - 1)

    return pl.pallas_call(
        functools.partial(_expert_kernel, tm=tm),
        grid_spec=pltpu.PrefetchScalarGridSpec(
            num_scalar_prefetch=3,
            grid=(m_pad // tm, nj),
            in_specs=[
                pl.BlockSpec(memory_space=pl.ANY),
                pl.BlockSpec((1, D_MODEL, tf), lambda i, j, be, nu, tok: (be[i], 0, jj(i, j, nu))),
                pl.BlockSpec((1, D_MODEL, tf), lambda i, j, be, nu, tok: (be[i], 0, jj(i, j, nu))),
                pl.BlockSpec((1, tf, D_MODEL), lambda i, j, be, nu, tok: (be[i], jj(i, j, nu), 0)),
            ],
            out_specs=pl.BlockSpec((tm, D_MODEL), lambda i, j, be, nu, tok: (i, 0)),
            scratch_shapes=[pltpu.VMEM((2, tm, D_MODEL), F32), pltpu.VMEM((tm, D_MODEL), BF16),
                            pltpu.SemaphoreType.DMA((2,))],
        ),
        out_shape=jax.ShapeDtypeStruct((m_pad, D_MODEL), F32),
        compiler_params=_cparams("arbitrary", "arbitrary"),
        name="experts",
    )(block_e, n_used, slot_tok, h2d, wg, wu, wd)


def _combine_kernel(dest_ref, x1_ref, gate_ref, ys_hbm, mod_ref, modn_ref, lnw_ref, lnb_ref,
                    x2_ref, u_ref, buf_ref, sems, *, tm):
    n_inner = pl.num_programs(1)
    tile = pl.program_id(0) * n_inner + pl.program_id(1)
    n_tiles = pl.num_programs(0) * n_inner
    sub = buf_ref.shape[3]
    groups = tm // sub
    slot = tile % 2

    def row(tok0, slot_, group, u, k):
        t = tok0 + group * sub + u
        return pltpu.make_async_copy(ys_hbm.at[pl.ds(dest_ref[2 * t + k], 1)],
                                     buf_ref.at[slot_, k, group, pl.ds(u, 1)], sems.at[slot_])

    def wait_tile(slot_):
        def wait(group, _):
            for u in range(sub):
                row(0, slot_, 0, 0, 0).wait()
                row(0, slot_, 0, 0, 1).wait()
            return 0
        lax.fori_loop(0, groups, wait, 0)

    @pl.when(tile == 0)
    def _first():
        def start(group, _):
            for u in range(sub):
                row(0, 0, group, u, 0).start()
                row(0, 0, group, u, 1).start()
            return 0
        lax.fori_loop(0, groups, start, 0)

    wait_tile(slot)
    nxt_tok0 = jnp.minimum(tile + 1, n_tiles - 1) * tm
    for group in range(groups):
        for u in range(sub):
            row(nxt_tok0, 1 - slot, group, u, 0).start()
            row(nxt_tok0, 1 - slot, group, u, 1).start()
    gate = gate_ref[0]
    f = (gate[:, 0:1] * buf_ref[slot, 0].reshape(tm, D_MODEL)
         + gate[:, 1:2] * buf_ref[slot, 1].reshape(tm, D_MODEL))
    _write_epilogue(x1_ref[0], f, mod_ref[0], modn_ref, lnw_ref, lnb_ref, x2_ref, u_ref)

    @pl.when(tile == n_tiles - 1)
    def _drain():
        wait_tile(1 - slot)


def _combine(dest_flat, x1, gates, ys, mod, modn, ln_w, ln_b):
    B, S, _ = x1.shape
    tm = min(TM_COMB, S)
    rows = pl.BlockSpec((1, tm, D_MODEL), lambda b, i, d: (b, i, 0))
    modspec = pl.BlockSpec((1, 6, D_MODEL), lambda b, i, d: (b, 0, 0))
    cst = pl.BlockSpec((1, D_MODEL), lambda b, i, d: (0, 0))
    return pl.pallas_call(
        functools.partial(_combine_kernel, tm=tm),
        grid_spec=pltpu.PrefetchScalarGridSpec(
            num_scalar_prefetch=1,
            grid=(B, S // tm),
            in_specs=[rows, pl.BlockSpec((1, tm, TOP_K), lambda b, i, d: (b, i, 0)),
                      pl.BlockSpec(memory_space=pl.ANY), modspec, modspec, cst, cst],
            out_specs=[rows, rows],
            scratch_shapes=[pltpu.VMEM((2, TOP_K, tm // SUBLANES, SUBLANES, D_MODEL), F32),
                            pltpu.SemaphoreType.DMA((2,))],
        ),
        out_shape=[
            jax.ShapeDtypeStruct((B, S, D_MODEL), F32),
            jax.ShapeDtypeStruct((B, S, D_MODEL), BF16),
        ],
        compiler_params=_cparams("arbitrary", "arbitrary"),
        name="combine",
    )(dest_flat, x1, gates, ys, mod, modn, ln_w, ln_b)


def _moe(x1, h, rw_pad, rb_pad, ltri, wg, wu, wd, mod, modn, ln_w, ln_b):
    B, S, _ = x1.shape
    n = B * S
    m = n * TOP_K
    rec, cnt = _router(h, rw_pad, rb_pad, ltri)
    rec = rec.reshape(n, LANES)
    e = rec[:, 0:TOP_K].astype(jnp.int32)
    gates = rec[:, TOP_K:2 * TOP_K]
    rank = rec[:, 2 * TOP_K:3 * TOP_K].astype(jnp.int32)
    counts = cnt[0, :N_EXPERTS].astype(jnp.int32)
    padded = (counts + TM_EXP - 1) // TM_EXP * TM_EXP
    pad_end = jnp.cumsum(padded)
    pad_start = pad_end - padded
    dest = (pad_start[e] + rank).reshape(m)
    n_blocks = m // TM_EXP + N_EXPERTS
    block_e = jnp.minimum(
        jnp.searchsorted(pad_end, jnp.arange(n_blocks, dtype=jnp.int32) * TM_EXP, side="right"),
        N_EXPERTS - 1).astype(jnp.int32)
    n_used = (pad_end[-1:] // TM_EXP).astype(jnp.int32)
    slot_tok = jnp.zeros((n_blocks * TM_EXP,), jnp.int32).at[dest].set(
        jnp.arange(m, dtype=jnp.int32) // TOP_K)
    ys = _experts(block_e, n_used, slot_tok, h.reshape(n, D_MODEL), wg, wu, wd)
    return _combine(dest, x1, gates.reshape(B, S, TOP_K), ys, mod, modn, ln_w, ln_b)


def _mla_weights(w_uq, w_ukv):
    L = w_uq.shape[0]
    H = MLA_HEADS
    half = QK_ROPE // 2
    wq = w_uq.reshape(L, Q_LORA, H, QK_NOPE + QK_ROPE)
    nope, rope = wq[..., :QK_NOPE], wq[..., QK_NOPE:]
    zpad = jnp.zeros((L, Q_LORA, H, HEAD_PAD - QK_NOPE - QK_ROPE), w_uq.dtype)
    wq_pad = jnp.concatenate([nope, rope, zpad], -1).reshape(L, Q_LORA, H * HEAD_PAD)
    rope_rot = jnp.concatenate([-rope[..., half:], rope[..., :half]], -1)
    wq_rot = jnp.concatenate([jnp.zeros_like(nope), rope_rot, zpad], -1).reshape(L, Q_LORA, H * HEAD_PAD)
    wkv = w_ukv.reshape(L, KV_LORA, H, QK_NOPE + V_HEAD)
    k_nope, v = wkv[..., :QK_NOPE], wkv[..., QK_NOPE:]
    wk_pad = jnp.concatenate(
        [k_nope, jnp.zeros((L, KV_LORA, H, HEAD_PAD - QK_NOPE), w_ukv.dtype)], -1
    ).reshape(L, KV_LORA, H * HEAD_PAD)
    wv = jnp.swapaxes(v.reshape(L, KV_LORA, MLA_DIM), 1, 2)
    return wq_pad.astype(BF16), wq_rot.astype(BF16), wk_pad.astype(BF16), wv.astype(BF16)


def _rope_placement():
    half = QK_ROPE // 2
    src = jnp.arange(QK_ROPE)[:, None]
    dst = jnp.arange(HEAD_PAD)[None, :]
    place = (dst == src + QK_NOPE).astype(BF16)
    rot = jnp.where((src < half) & (dst == src + QK_NOPE + half), 1.0,
                    jnp.where((src >= half) & (dst == src + QK_NOPE - half), -1.0, 0.0))
    return place, rot.astype(BF16)


def kernel(x, c, positions, w_ada, b_ada, w_in, shift_mu, decay_w0, decay_up, aaa_a0, aaa_up, gate_up, k_k, k_a, r_k, gn_w, gn_b, q_norm_w, w_uq, kv_norm_w, w_ukv, p_a, p_b, w_o, ln1_w, ln1_b, ln2_w, ln2_b, ffn_w_gate, ffn_w_up, ffn_w_down, router_w, router_b, moe_w_gate, moe_w_up, moe_w_down):
    B, S, _ = x.shape
    n = B * S
    L = DEPTH
    assert S % RW_CHUNK == 0 and S % LANES == 0

    mod = _ada(c, w_ada, b_ada).reshape(L, B, 6, D_MODEL)
    cos_t, sin_t = _rope_tables(positions.astype(F32).reshape(B, S, 1))

    w_shift = w_in[:, :, :SHIFT_DIM].astype(BF16)
    w_lat = jnp.pad(w_in[:, :, SHIFT_DIM:SHIFT_DIM + LAT_DIM],
                    ((0, 0), (0, 0), (0, LAT_PAD - LAT_DIM))).astype(BF16)
    w_gates = w_in[:, :, SHIFT_DIM + LAT_DIM:].astype(BF16)
    wq_pad, wq_rot, wk_pad, wv = _mla_weights(w_uq, w_ukv)
    place, placer = _rope_placement()
    lane_head = jnp.arange(RW_DIM) // RW_HEAD
    hsum = (lane_head[:, None] == lane_head[None, :]).astype(BF16)
    row1 = lambda a, l: a[l].reshape(1, -1)
    tm_r = min(TM_ROUTE, S)
    ltri = (jnp.arange(tm_r)[None, :] < jnp.arange(tm_r)[:, None]).astype(BF16)
    rw_pad = jnp.pad(router_w, ((0, 0), (0, 0), (0, LANES - N_EXPERTS)))
    rb_pad = jnp.pad(router_b, ((0, 0), (0, LANES - N_EXPERTS)), constant_values=NEG_BIG)

    u = _lnmod(x, mod[0])
    for l in range(L):
        u2 = u.reshape(n, D_MODEL)
        ps, lat, sg = _proj(u2, w_shift[l], w_lat[l], w_gates[l])
        ps = ps.reshape(B, S, SHIFT_DIM)
        lat = lat.reshape(B, S, LAT_PAD)
        sg = sg.reshape(B, S, 2 * D_MODEL)
        ya = _rwkv(ps, row1(shift_mu, l), row1(decay_w0, l), decay_up[l].astype(BF16), row1(aaa_a0, l),
                   aaa_up[l].astype(BF16), gate_up[l].astype(BF16), row1(k_k, l), row1(k_a, l),
                   row1(r_k, l), row1(gn_w, l), row1(gn_b, l), hsum)
        q, k, v = _mla_prep(lat, cos_t, sin_t, row1(q_norm_w, l), wq_pad[l], wq_rot[l], row1(kv_norm_w, l),
                            wk_pad[l], wv[l], place, placer)
        yb = _attn(q, k, v)
        moe_layer = l % 2 == 1
        x1, h = _merge_out(x, ya, yb, sg, p_a[l].astype(BF16), p_b[l].astype(BF16), w_o[l].astype(BF16),
                           mod[l], row1(ln1_w, l), row1(ln1_b, l), F32 if moe_layer else BF16)
        modn = mod[min(l + 1, L - 1)]
        if moe_layer:
            x, u = _moe(x1, h, rw_pad[l // 2], rb_pad[l // 2].reshape(1, LANES), ltri,
                        moe_w_gate[l // 2].astype(BF16), moe_w_up[l // 2].astype(BF16),
                        moe_w_down[l // 2].astype(BF16), mod[l], modn, row1(ln2_w, l), row1(ln2_b, l))
        else:
            x, u = _ffn(x1, h, ffn_w_gate[l // 2].astype(BF16), ffn_w_up[l // 2].astype(BF16),
                        ffn_w_down[l // 2].astype(BF16), mod[l], modn, row1(ln2_w, l), row1(ln2_b, l))
    return x
```
